```python
import jax, jax.numpy as jnp
from jax import lax
import numpy as np

D_MODEL = 2048
BATCH = 1
SEQ = 16384
DEPTH = 4
DEC_BATCH = 32
DEC_SEQ = 64
PAST_LEN = 4096

CHUNK = 64
N_META = 16
H_A = 8
HD_A = 128
C_ATT = H_A * HD_A
H_IDX = 8
D_IDX = 64
MAX_SEL = 256
QBLK = 128
H_B = 8
HD_B = 128
C_DN = H_B * HD_B
CONV_W = 4
DN_CHUNK = 64
D_FF = 5504
DEEP_ALPHA = (2 * DEPTH) ** 0.25
DEEP_BETA = (8 * DEPTH) ** -0.25
LN_EPS = 1e-5
NORM_EPS = 1e-6
BIG_CHUNK = 2 ** 30
IN_SIZES = (C_ATT, C_ATT, C_ATT, H_IDX * D_IDX, D_IDX, H_IDX, 3 * C_DN, C_DN, H_B, H_B, D_MODEL, D_MODEL)
N_IN = 3 * C_ATT + H_IDX * D_IDX + D_IDX + H_IDX + 4 * C_DN + 2 * H_B + 2 * D_MODEL

kernel_name = 'hybrid_dsa_gdn_stream'


def split_cols(a, sizes):
    out, start = [], 0
    for s in sizes:
        out.append(a[..., start:start + s])
        start += s
    return out


def layer_norm(x, g, b):
    xf = x.astype(jnp.float32)
    mu = jnp.mean(xf, axis=-1, keepdims=True)
    var = jnp.mean(jnp.square(xf - mu), axis=-1, keepdims=True)
    y = (xf - mu) * lax.rsqrt(var + LN_EPS) * g.astype(jnp.float32) + b.astype(jnp.float32)
    return y.astype(x.dtype)


def l2_normalize(x):
    xf = x.astype(jnp.float32)
    return xf * lax.rsqrt(jnp.sum(xf * xf, axis=-1, keepdims=True) + NORM_EPS)


def swiglu(x, w_gu, w_down):
    gate, up = split_cols(x @ w_gu, (D_FF, D_FF))
    return (jax.nn.silu(gate) * up) @ w_down


def dsa_attention(q, qi, wi, qpos, qchk, k, v, ki, kpos, kchk, n_sel):
    B, T = q.shape[0], q.shape[1]
    blk = min(QBLK, T)
    nb = -(-T // blk)
    pad = nb * blk - T

    def to_blocks(a):
        a = jnp.pad(a, [(0, 0), (0, pad)] + [(0, 0)] * (a.ndim - 2))
        return jnp.moveaxis(a.reshape((B, nb, blk) + a.shape[2:]), 1, 0)

    qpos_b = jnp.pad(qpos, (0, pad)).reshape(nb, blk)
    qchk_b = jnp.pad(qchk, (0, pad), constant_values=BIG_CHUNK).reshape(nb, blk)
    slopes = jnp.exp2(-(8.0 / H_A) * jnp.arange(1, H_A + 1, dtype=jnp.float32))

    def one_block(args):
        qb, qib, wib, qp, qc = args
        allowed = kchk[None, :] <= qc[:, None]
        s = jnp.einsum('bqhd,bld->bqhl', qib, ki).astype(jnp.float32) * (D_IDX ** -0.5)
        score = jnp.einsum('bqh,bqhl->bql', wib.astype(jnp.float32), jax.nn.relu(s))
        score = jnp.where(allowed[None], score, -jnp.inf)
        top_val, sel = lax.top_k(score, n_sel)
        valid = jnp.isfinite(top_val)
        ks = jax.vmap(lambda kk, ii: kk[ii])(k, sel)
        vs = jax.vmap(lambda vv, ii: vv[ii])(v, sel)
        dist = jnp.abs(qp[None, :, None] - kpos[sel]).astype(jnp.float32)
        logits = jnp.einsum('bqhd,bqshd->bhqs', qb, ks).astype(jnp.float32) * (HD_A ** -0.5)
        logits = logits - slopes[None, :, None, None] * dist[:, None]
        logits = jnp.where(valid[:, None], logits, -jnp.inf)
        p = jax.nn.softmax(logits, axis=-1)
        return jnp.einsum('bhqs,bqshd->bqhd', p.astype(vs.dtype), vs)

    out = lax.map(one_block, (to_blocks(q), to_blocks(qi), to_blocks(wi), qpos_b, qchk_b))
    out = jnp.moveaxis(out, 0, 1).reshape(B, nb * blk, H_A, HD_A)
    return out[:, :T]


def gated_delta_rule(q, k, v, log_a, beta, s0):
    B, T, H, DK = q.shape
    DV = v.shape[-1]
    n = -(-T // DN_CHUNK)
    pad = n * DN_CHUNK - T

    def blocks(a):
        a = jnp.pad(a.astype(jnp.float32), [(0, 0), (0, pad)] + [(0, 0)] * (a.ndim - 2))
        a = a.reshape((B, n, DN_CHUNK) + a.shape[2:])
        return jnp.swapaxes(jnp.moveaxis(a, 1, 0), 2, 3)

    causal = jnp.tril(jnp.ones((DN_CHUNK, DN_CHUNK), dtype=bool))
    strict = jnp.tril(jnp.ones((DN_CHUNK, DN_CHUNK), dtype=bool), k=-1)
    eye = jnp.eye(DN_CHUNK, dtype=jnp.float32)

    def step(s, blk):
        qc, kc, vc, lac, bc = blk
        g = jnp.cumsum(lac, axis=-1)
        decay = jnp.exp(jnp.where(causal, g[..., :, None] - g[..., None, :], -jnp.inf))
        kk = jnp.einsum('bhid,bhjd->bhij', kc, kc)
        m = eye + jnp.where(strict, bc[..., :, None] * kk * decay, 0.0)
        rhs = jnp.concatenate([vc * bc[..., None], kc * (bc * jnp.exp(g))[..., None]], axis=-1)
        sol = lax.linalg.triangular_solve(m, rhs, left_side=True, lower=True, unit_diagonal=True)
        u = sol[..., :DV] - jnp.einsum('bhcd,bhde->bhce', sol[..., DV:], s)
        qk = jnp.einsum('bhid,bhjd->bhij', qc, kc) * decay
        o = jnp.einsum('bhcd,bhde->bhce', qc * jnp.exp(g)[..., None], s) + jnp.einsum('bhij,bhje->bhie', qk, u)
        g_last = g[..., -1:]
        s = s * jnp.exp(g_last)[..., None] + jnp.einsum('bhcd,bhce->bhde', kc * jnp.exp(g_last - g)[..., None], u)
        return s, o

    s_fin, o = lax.scan(step, s0.astype(jnp.float32), (blocks(q), blocks(k), blocks(v), blocks(log_a), blocks(beta)))
    o = jnp.moveaxis(jnp.swapaxes(o, 2, 3), 0, 1).reshape(B, n * DN_CHUNK, H, DV)
    return o[:, :T], s_fin


def hybrid_mixer(h, qpos, qchk, kpos, kchk, k_past, v_past, ki_past, conv_prev, s0, n_sel,
                 w_in, conv_w, idx_ln_g, idx_ln_b, a_log, dt_bias, dn_norm_g, w_oa, w_ob, w_out):
    B, T, _ = h.shape
    q_a, k_a, v_a, q_i, k_i, w_i, qkv_b, g_b, b_b, a_b, gate_a, gate_b = split_cols(h @ w_in, IN_SIZES)
    q_a = q_a.reshape(B, T, H_A, HD_A)
    k_a = k_a.reshape(B, T, H_A, HD_A)
    v_a = v_a.reshape(B, T, H_A, HD_A)
    q_i = q_i.reshape(B, T, H_IDX, D_IDX)
    k_i = layer_norm(k_i, idx_ln_g, idx_ln_b)
    w_i = w_i * (H_IDX ** -0.5)
    if k_past is None:
        keys, vals, ikeys = k_a, v_a, k_i
    else:
        keys = jnp.concatenate([k_past, k_a], axis=1)
        vals = jnp.concatenate([v_past, v_a], axis=1)
        ikeys = jnp.concatenate([ki_past, k_i], axis=1)
    o_a = dsa_attention(q_a, q_i, w_i, qpos, qchk, keys, vals, ikeys, kpos, kchk, n_sel).reshape(B, T, C_ATT)
    xc = jnp.concatenate([conv_prev, qkv_b], axis=1)
    conv = xc[:, 0:T] * conv_w[0]
    for i in range(1, CONV_W):
        conv = conv + xc[:, i:i + T] * conv_w[i]
    q_b, k_b, v_b = split_cols(jax.nn.silu(conv), (C_DN, C_DN, C_DN))
    q_b = l2_normalize(q_b.reshape(B, T, H_B, HD_B)) * (HD_B ** -0.5)
    k_b = l2_normalize(k_b.reshape(B, T, H_B, HD_B))
    v_b = v_b.reshape(B, T, H_B, HD_B)
    log_a = -jnp.exp(a_log.astype(jnp.float32)) * jax.nn.softplus(a_b.astype(jnp.float32) + dt_bias.astype(jnp.float32))
    beta = jax.nn.sigmoid(b_b.astype(jnp.float32))
    o_b, s_new = gated_delta_rule(q_b, k_b, v_b, log_a, beta, s0)
    o_b = o_b * lax.rsqrt(jnp.mean(o_b * o_b, axis=-1, keepdims=True) + NORM_EPS) * dn_norm_g.astype(jnp.float32)
    o_b = o_b * jax.nn.silu(g_b.reshape(B, T, H_B, HD_B).astype(jnp.float32))
    o_b = o_b.reshape(B, T, C_DN).astype(h.dtype)
    y = jax.nn.sigmoid(gate_a) * (o_a @ w_oa) + jax.nn.sigmoid(gate_b) * (o_b @ w_ob)
    return y @ w_out, (k_a, v_a, k_i, xc[:, -(CONV_W - 1):], s_new.astype(s0.dtype))


def trunk_layer(x, qpos, qchk, kpos, kchk, k_past, v_past, ki_past, conv_prev, s0, n_sel,
                w_ff_gu, w_ff_down, ln_g, ln_b, w_in, conv_w, idx_ln_g, idx_ln_b,
                a_log, dt_bias, dn_norm_g, w_oa, w_ob, w_out):
    x = layer_norm(DEEP_ALPHA * x + 0.5 * swiglu(x, w_ff_gu[0], w_ff_down[0]), ln_g[0], ln_b[0])
    mix, new_state = hybrid_mixer(x, qpos, qchk, kpos, kchk, k_past, v_past, ki_past, conv_prev, s0, n_sel,
                                  w_in, conv_w, idx_ln_g, idx_ln_b, a_log, dt_bias, dn_norm_g, w_oa, w_ob, w_out)
    x = layer_norm(DEEP_ALPHA * x + mix, ln_g[1], ln_b[1])
    x = layer_norm(DEEP_ALPHA * x + 0.5 * swiglu(x, w_ff_gu[1], w_ff_down[1]), ln_g[2], ln_b[2])
    return x, new_state


def setup_inputs(seed: int = 0) -> dict:
    key = jax.random.key(seed)
    ks = jax.random.split(key, 24)
    f32 = jnp.float32
    nrm = jax.random.normal
    x_prompt = nrm(ks[0], (BATCH, SEQ, D_MODEL), f32)
    x_sample = nrm(ks[1], (DEC_BATCH, DEC_SEQ, D_MODEL), f32)
    cache_k = nrm(ks[2], (DEPTH, DEC_BATCH, PAST_LEN, H_A, HD_A), f32)
    cache_v = nrm(ks[3], (DEPTH, DEC_BATCH, PAST_LEN, H_A, HD_A), f32)
    cache_kidx = nrm(ks[4], (DEPTH, DEC_BATCH, PAST_LEN, D_IDX), f32)
    state_conv = nrm(ks[5], (DEPTH, DEC_BATCH, CONV_W - 1, 3 * C_DN), f32)
    state_dn = 0.1 * nrm(ks[6], (DEPTH, DEC_BATCH, H_B, HD_B, HD_B), f32)
    meta = nrm(ks[7], (N_META, D_MODEL), f32)
    w_ff_gu = nrm(ks[8], (DEPTH, 2, D_MODEL, 2 * D_FF), f32) * D_MODEL ** -0.5
    w_ff_down = nrm(ks[9], (DEPTH, 2, D_FF, D_MODEL), f32) * (D_FF ** -0.5 * DEEP_BETA)
    ln_g = 1.0 + 0.02 * nrm(ks[10], (DEPTH, 3, D_MODEL), f32)
    ln_b = 0.02 * nrm(ks[11], (DEPTH, 3, D_MODEL), f32)
    col_scale = jnp.concatenate([
        jnp.ones((2 * C_ATT,), f32), jnp.full((C_ATT,), DEEP_BETA, f32),
        jnp.ones((H_IDX * D_IDX + D_IDX + H_IDX + 2 * C_DN,), f32), jnp.full((C_DN,), DEEP_BETA, f32),
        jnp.ones((C_DN + 2 * H_B + 2 * D_MODEL,), f32)])
    w_in = nrm(ks[12], (DEPTH, D_MODEL, N_IN), f32) * D_MODEL ** -0.5 * col_scale
    conv_w = nrm(ks[13], (DEPTH, CONV_W, 3 * C_DN), f32) * CONV_W ** -0.5
    idx_ln_g = 1.0 + 0.02 * nrm(ks[14], (DEPTH, D_IDX), f32)
    idx_ln_b = 0.02 * nrm(ks[15], (DEPTH, D_IDX), f32)
    a_log = jnp.log(jax.random.uniform(ks[16], (DEPTH, H_B), f32, minval=1.0, maxval=16.0))
    dt = jnp.exp(jax.random.uniform(ks[17], (DEPTH, H_B), f32, minval=float(np.log(1e-3)), maxval=float(np.log(1e-1))))
    dt_bias = dt + jnp.log(-jnp.expm1(-dt))
    dn_norm_g = 1.0 + 0.02 * nrm(ks[18], (DEPTH, HD_B), f32)
    w_oa = nrm(ks[19], (DEPTH, C_ATT, D_MODEL), f32) * (C_ATT ** -0.5 * DEEP_BETA)
    w_ob = nrm(ks[20], (DEPTH, C_DN, D_MODEL), f32) * (C_DN ** -0.5 * DEEP_BETA)
    w_out = nrm(ks[21], (DEPTH, D_MODEL, D_MODEL), f32) * (D_MODEL ** -0.5 * DEEP_BETA)
    return {'x_prompt': x_prompt, 'x_sample': x_sample, 'cache_k': cache_k, 'cache_v': cache_v,
            'cache_kidx': cache_kidx, 'state_conv': state_conv, 'state_dn': state_dn, 'meta': meta,
            'w_ff_gu': w_ff_gu, 'w_ff_down': w_ff_down, 'ln_g': ln_g, 'ln_b': ln_b, 'w_in': w_in,
            'conv_w': conv_w, 'idx_ln_g': idx_ln_g, 'idx_ln_b': idx_ln_b, 'a_log': a_log,
            'dt_bias': dt_bias, 'dn_norm_g': dn_norm_g, 'w_oa': w_oa, 'w_ob': w_ob, 'w_out': w_out}


def reference(x_prompt, x_sample, cache_k, cache_v, cache_kidx, state_conv, state_dn, meta,
              w_ff_gu, w_ff_down, ln_g, ln_b, w_in, conv_w, idx_ln_g, idx_ln_b,
              a_log, dt_bias, dn_norm_g, w_oa, w_ob, w_out):
    dtype = x_prompt.dtype
    B, S = x_prompt.shape[0], x_prompt.shape[1]
    xp = jnp.concatenate([jnp.broadcast_to(meta[None].astype(dtype), (B, N_META, meta.shape[-1])), x_prompt], axis=1)
    pos_p = jnp.arange(N_META + S, dtype=jnp.int32)
    chk_p = jnp.where(pos_p < N_META, -1, (pos_p - N_META) // CHUNK)
    n_sel_p = min(MAX_SEL, S // 4)
    conv0 = jnp.zeros((B, CONV_W - 1, 3 * C_DN), dtype)
    s0 = jnp.zeros((B, H_B, HD_B, HD_B), dtype)
    Ts = x_sample.shape[1]
    P = cache_k.shape[2]
    kpos_s = jnp.arange(P + Ts, dtype=jnp.int32)
    kchk_s = kpos_s // CHUNK
    qpos_s = kpos_s[P:]
    qchk_s = kchk_s[P:]
    n_sel_s = min(MAX_SEL, (P + Ts) // 4)
    xs = x_sample
    kp, vp, kip, cp, sp = [], [], [], [], []
    ksm, vsm, kism, csm, ssm = [], [], [], [], []
    for l in range(DEPTH):
        lw = (w_ff_gu[l], w_ff_down[l], ln_g[l], ln_b[l], w_in[l], conv_w[l], idx_ln_g[l], idx_ln_b[l],
              a_log[l], dt_bias[l], dn_norm_g[l], w_oa[l], w_ob[l], w_out[l])
        xp, st_p = trunk_layer(xp, pos_p, chk_p, pos_p, chk_p, None, None, None, conv0, s0, n_sel_p, *lw)
        xs, st_s = trunk_layer(xs, qpos_s, qchk_s, kpos_s, kchk_s, cache_k[l], cache_v[l], cache_kidx[l],
                               state_conv[l], state_dn[l], n_sel_s, *lw)
        kp.append(st_p[0]); vp.append(st_p[1]); kip.append(st_p[2]); cp.append(st_p[3]); sp.append(st_p[4])
        ksm.append(st_s[0]); vsm.append(st_s[1]); kism.append(st_s[2]); csm.append(st_s[3]); ssm.append(st_s[4])
    y_prompt = xp[:, N_META:]
    return (y_prompt, xs,
            jnp.stack(kp), jnp.stack(vp), jnp.stack(kip), jnp.stack(cp), jnp.stack(sp),
            jnp.stack(ksm), jnp.stack(vsm), jnp.stack(kism), jnp.stack(csm), jnp.stack(ssm))
```

```python
import functools

import numpy as np
import jax
import jax.numpy as jnp
from jax import lax
from jax.experimental import pallas as pl
from jax.experimental.pallas import tpu as pltpu

CHUNK = 64
MAX_SEL = 256
DN_C = 64
LN_EPS = 1e-5
NORM_EPS = 1e-6

LANES = 128
SUBLANES = 8
KC = 256
FF_ALIGN = 512
VMEM_LIMIT = 56 << 20

ROW_PAD = 1280
TM_FFN = 640
TM_MM = 1280
TN_MM = 1024
TN_MERGE = 512
TQ = 256
TK_PROMPT = 1280
TK_CACHE = 1024

BF = jnp.bfloat16
F32 = jnp.float32
I32 = jnp.int32
INT_MIN = -(2 ** 31)
NEG = -1e30
NT_DIMS = (((1,), (1,)), ((), ()))
TN_DIMS = (((0,), (0,)), ((), ()))


def _cparams(sem):
    return pltpu.CompilerParams(dimension_semantics=sem, vmem_limit_bytes=VMEM_LIMIT)


def _sigmoid(x):
    return 1.0 / (1.0 + jnp.exp(-x))


def _layer_norm(y, g, b):
    mu = jnp.mean(y, axis=-1, keepdims=True)
    d = y - mu
    var = jnp.mean(d * d, axis=-1, keepdims=True)
    return d * lax.rsqrt(var + LN_EPS) * g + b


def _pick_tile(n, pref):
    if n <= pref:
        return n
    for t in range(pref, 15, -16):
        if n % t == 0:
            return t
    return n


def _ffn_kernel(x_ref, wg_ref, wu_ref, wd_ref, g_ref, b_ref, o_ref, ob_ref, xb_ref, acc_ref, *, alpha, nf, rs):
    j = pl.program_id(1)

    @pl.when(j == 0)
    def _():
        xb_ref[...] = x_ref[...].astype(BF)
        acc_ref[...] = jnp.zeros_like(acc_ref)

    xb = xb_ref[...]
    hg = jnp.dot(xb, wg_ref[...], preferred_element_type=F32)
    hu = jnp.dot(xb, wu_ref[...], preferred_element_type=F32)
    a = (hg * _sigmoid(hg) * hu).astype(BF)
    acc_ref[...] += jnp.dot(a, wd_ref[...], preferred_element_type=F32)

    @pl.when(j == nf - 1)
    def _():
        def slab(r, carry):
            rows = pl.ds(pl.multiple_of(r * rs, rs), rs)
            y = alpha * x_ref[rows, :] + 0.5 * acc_ref[rows, :]
            z = _layer_norm(y, g_ref[...], b_ref[...])
            o_ref[rows, :] = z
            ob_ref[rows, :] = z.astype(BF)
            return carry
        lax.fori_loop(0, x_ref.shape[0] // rs, slab, 0)


def ffn_ln(x, wgu, wd, g, b, l, s, alpha, tm, tf):
    M, D = x.shape
    nf = wd.shape[2] // tf
    rs = 64 if tm % 64 == 0 else 16
    return pl.pallas_call(
        functools.partial(_ffn_kernel, alpha=alpha, nf=nf, rs=rs),
        grid=(M // tm, nf),
        in_specs=[
            pl.BlockSpec((tm, D), lambda i, j: (i, 0)),
            pl.BlockSpec((None, None, None, D, tf), lambda i, j: (l, s, 0, 0, j)),
            pl.BlockSpec((None, None, None, D, tf), lambda i, j: (l, s, 1, 0, j)),
            pl.BlockSpec((None, None, tf, D), lambda i, j: (l, s, j, 0)),
            pl.BlockSpec((1, D), lambda i, j: (0, 0)),
            pl.BlockSpec((1, D), lambda i, j: (0, 0)),
        ],
        out_specs=[pl.BlockSpec((tm, D), lambda i, j: (i, 0)), pl.BlockSpec((tm, D), lambda i, j: (i, 0))],
        out_shape=[jax.ShapeDtypeStruct((M, D), F32), jax.ShapeDtypeStruct((M, D), BF)],
        scratch_shapes=[pltpu.VMEM((tm, D), BF), pltpu.VMEM((tm, D), F32)],
        compiler_params=_cparams(("parallel", "arbitrary")),
        name="ffn_ln",
    )(x, wgu, wgu, wd, g, b)


def _mm_kernel(x_ref, w_ref, o_ref):
    o_ref[...] = jnp.dot(x_ref[...], w_ref[...], preferred_element_type=F32)


def matmul(xb, w, l, tm, tn):
    M, D = xb.shape
    N = w.shape[2]
    return pl.pallas_call(
        _mm_kernel,
        grid=(M // tm, N // tn),
        in_specs=[pl.BlockSpec((tm, D), lambda i, j: (i, 0)),
                  pl.BlockSpec((None, D, tn), lambda i, j: (l, 0, j))],
        out_specs=pl.BlockSpec((tm, tn), lambda i, j: (i, j)),
        out_shape=jax.ShapeDtypeStruct((M, N), F32),
        compiler_params=_cparams(("parallel", "arbitrary")),
        name="in_proj",
    )(xb, w)


def _idx_kernel(x_ref, w_ref, lng_ref, lnb_ref, alog_ref, dtb_ref, qi_ref, ki_ref, sm_ref, *, n_qi, d_idx, h_idx, h_b):
    r = jnp.dot(x_ref[...], w_ref[...], preferred_element_type=F32)
    qi_ref[...] = r[:, :n_qi].astype(BF)
    t = r[:, n_qi:]
    lane = lax.broadcasted_iota(I32, t.shape, 1)
    isk = lane < d_idx
    mu = jnp.sum(jnp.where(isk, t, 0.0), axis=-1, keepdims=True) / d_idx
    d = jnp.where(isk, t - mu, 0.0)
    var = jnp.sum(d * d, axis=-1, keepdims=True) / d_idx
    kn = d * lax.rsqrt(var + LN_EPS) * lng_ref[...] + lnb_ref[...]
    wv = t * (h_idx ** -0.5)
    beta = _sigmoid(t)
    z = t + dtb_ref[...]
    softplus = jnp.maximum(z, 0.0) + jnp.log(1.0 + jnp.exp(-jnp.abs(z)))
    la = -jnp.exp(alog_ref[...]) * softplus
    w_off, b_off, a_off = d_idx, d_idx + h_idx, d_idx + h_idx + h_b
    out = jnp.where(isk, kn,
                    jnp.where(lane < b_off, wv,
                              jnp.where(lane < a_off, beta,
                                        jnp.where(lane < a_off + h_b, la, 0.0))))
    sm_ref[...] = out
    ki_ref[...] = kn[:, :d_idx].astype(BF)


def idx_proj(xb, w, lng, lnb, alog, dtb, l, tm, d_idx, h_idx, h_b):
    M, D = xb.shape
    N = w.shape[2]
    n_qi = N - LANES
    vec = pl.BlockSpec((1, LANES), lambda i: (0, 0))
    return pl.pallas_call(
        functools.partial(_idx_kernel, n_qi=n_qi, d_idx=d_idx, h_idx=h_idx, h_b=h_b),
        grid=(M // tm,),
        in_specs=[pl.BlockSpec((tm, D), lambda i: (i, 0)),
                  pl.BlockSpec((None, D, N), lambda i: (l, 0, 0)),
                  vec, vec, vec, vec],
        out_specs=[pl.BlockSpec((tm, n_qi), lambda i: (i, 0)),
                   pl.BlockSpec((tm, d_idx), lambda i: (i, 0)),
                   pl.BlockSpec((tm, LANES), lambda i: (i, 0))],
        out_shape=[jax.ShapeDtypeStruct((M, n_qi), BF),
                   jax.ShapeDtypeStruct((M, d_idx), BF),
                   jax.ShapeDtypeStruct((M, LANES), F32)],
        compiler_params=_cparams(("parallel",)),
        name="idx_proj",
    )(xb, w, lng, lnb, alog, dtb)


def _sort_key(s):
    b = pltpu.bitcast(s + 0.0, I32)
    return b ^ ((b >> 31) & 0x7FFFFFFF)


def _index_scores(ki_chunk, qi_ref, w_rows, h_idx, d_idx):
    acc = None
    for h in range(h_idx):
        s = lax.dot_general(ki_chunk, qi_ref[:, h * d_idx:(h + 1) * d_idx], NT_DIMS, preferred_element_type=F32)
        term = w_rows[h:h + 1, :] * jnp.maximum(s, 0.0)
        acc = term if acc is None else acc + term
    return acc


def _count_chunks(S_ref, nck, tq, pred):
    def body(c, cnt):
        s = S_ref[c]
        ones = jnp.where(pred(s, c), 1, 0).astype(I32)
        return cnt + ones.reshape(s.shape[0] // SUBLANES, SUBLANES, tq).sum(axis=0)
    cnt8 = lax.fori_loop(0, nck, body, jnp.zeros((SUBLANES, tq), I32))
    return cnt8.sum(axis=0, keepdims=True)


def _select_threshold(S_ref, nck, n_sel, tq, idx_bits, thr_ref, cut_ref):
    def bit_body(i, carry):
        thr, cnt_thr = carry
        cand = thr + jnp.left_shift(jnp.int32(1), 31 - i)
        cnt = _count_chunks(S_ref, nck, tq, lambda s, c: s >= cand)
        ok = cnt >= n_sel
        return jnp.where(ok, cand, thr), jnp.where(ok, cnt, cnt_thr)

    thr0 = jnp.full((1, tq), INT_MIN, I32)
    thr, cnt_thr = lax.fori_loop(0, 32, bit_body, (thr0, jnp.zeros((1, tq), I32)))
    has_tie = (thr > INT_MIN) & (cnt_thr > n_sel)
    thr_ref[...] = jnp.maximum(thr, INT_MIN + 1)
    cut_ref[...] = jnp.full((1, tq), 2 ** 30, I32)

    @pl.when(jnp.max(has_tie.astype(I32)) > 0)
    def _():
        kc = S_ref.shape[1]
        need = n_sel - _count_chunks(S_ref, nck, tq, lambda s, c: s > thr)

        def idx_body(i, cut):
            cand = cut + jnp.left_shift(jnp.int32(1), idx_bits - 1 - i)

            def pred(s, c):
                kidx = c * kc + lax.broadcasted_iota(I32, s.shape, 0)
                return (s == thr) & (kidx < cand)
            cnt = _count_chunks(S_ref, nck, tq, pred)
            return jnp.where(cnt < need, cand, cut)

        cut = lax.fori_loop(0, idx_bits, idx_body, jnp.zeros((1, tq), I32))
        cut_ref[...] = jnp.where(has_tie, cut, 2 ** 30)


def _flash_update(keys, kidx0, k_chunk, vt_chunk, q_ref, dist, thr, cut, m_ref, l_ref, acc_ref, n_heads, hd):
    kidx = kidx0 + lax.broadcasted_iota(I32, keys.shape, 0)
    sel = (keys > thr) | ((keys == thr) & (kidx <= cut))
    for h in range(n_heads):
        cols = slice(h * hd, (h + 1) * hd)
        slope = float(2.0 ** (-(8.0 / n_heads) * (h + 1)))
        lt = lax.dot_general(k_chunk[:, cols], q_ref[:, cols], NT_DIMS, preferred_element_type=F32)
        lt = jnp.where(sel, lt - slope * dist, NEG)
        m_old = m_ref[h]
        m_new = jnp.maximum(m_old, jnp.max(lt, axis=0, keepdims=True))
        p = jnp.exp(lt - m_new)
        a = jnp.exp(m_old - m_new)
        l_ref[h] = a * l_ref[h] + jnp.sum(p, axis=0, keepdims=True)
        m_ref[h] = m_new
        pv = jnp.dot(vt_chunk[cols, :], p.astype(BF), preferred_element_type=F32)
        acc_ref[h] = a * acc_ref[h] + pv


def _flash_init(m_ref, l_ref, acc_ref):
    m_ref[...] = jnp.full(m_ref.shape, NEG, F32)
    l_ref[...] = jnp.zeros(l_ref.shape, F32)
    acc_ref[...] = jnp.zeros(acc_ref.shape, F32)


def _flash_finish(o_ref, l_ref, acc_ref, n_heads, hd, tq):
    for h in range(n_heads):
        ot = acc_ref[h] / l_ref[h]
        if tq % LANES:
            ot = jnp.concatenate([ot, jnp.zeros((hd, LANES - tq % LANES), F32)], axis=1)
        o_ref[:, h * hd:(h + 1) * hd] = ot.T[:tq].astype(BF)


def _dsa_prompt_kernel(qmap, kmap, nckq, q_ref, k_ref, vt_ref, qi_ref, ki_ref, smt_ref, o_ref,
                       S_ref, thr_ref, cut_ref, qs_ref, w_ref, m_ref, l_ref, acc_ref,
                       *, n_heads, hd, h_idx, d_idx, n_meta, t_valid, n_sel, cpb, idx_bits):
    p = pl.program_id(0)
    qb, kb = qmap[p], kmap[p]
    nck = nckq[qb]
    tq = q_ref.shape[0]
    qpos = qb * tq + lax.broadcasted_iota(I32, (KC, tq), 1)

    @pl.when(kb == 0)
    def _():
        qs_ref[...] = (q_ref[...] * (hd ** -0.5)).astype(BF)
        w_ref[...] = smt_ref[d_idx:d_idx + SUBLANES, :] * (d_idx ** -0.5)
        qchk = jnp.maximum((qpos - n_meta) >> 6, -1)

        def score_body(c, carry):
            k0 = pl.multiple_of(c * KC, KC)
            sc = _index_scores(ki_ref[pl.ds(k0, KC), :], qi_ref, w_ref[...], h_idx, d_idx)
            kpos = k0 + lax.broadcasted_iota(I32, (KC, tq), 0)
            allowed = (jnp.maximum((kpos - n_meta) >> 6, -1) <= qchk) & (kpos < t_valid)
            S_ref[c] = jnp.where(allowed, _sort_key(sc), INT_MIN)
            return carry
        lax.fori_loop(0, nck, score_body, 0)
        _select_threshold(S_ref, nck, n_sel, tq, idx_bits, thr_ref, cut_ref)
        _flash_init(m_ref, l_ref, acc_ref)

    thr, cut = thr_ref[...], cut_ref[...]

    def chunk_body(c, carry):
        cg = kb * cpb + c
        rows = pl.ds(pl.multiple_of(c * KC, KC), KC)
        kpos = cg * KC + lax.broadcasted_iota(I32, (KC, tq), 0)
        dist = jnp.abs(qpos - kpos).astype(F32)
        _flash_update(S_ref[cg], cg * KC, k_ref[rows, :].astype(BF), vt_ref[c].astype(BF), qs_ref, dist,
                      thr, cut, m_ref, l_ref, acc_ref, n_heads, hd)
        return carry
    lax.fori_loop(0, jnp.minimum(cpb, nck - kb * cpb), chunk_body, 0)

    @pl.when((kb + 1) * cpb >= nck)
    def _():
        _flash_finish(o_ref, l_ref, acc_ref, n_heads, hd, tq)


def dsa_prompt(proj, vt3, qi, ki, smt, *, t_valid, n_meta, n_heads, hd, h_idx, d_idx, n_sel, tq, tk):
    Mp = proj.shape[0]
    C = n_heads * hd
    cpb = tk // KC
    nq = Mp // tq
    qlast = np.minimum(np.arange(nq) * tq + tq - 1, t_valid - 1)
    kend = np.minimum(t_valid, n_meta + CHUNK * (np.maximum((qlast - n_meta) // CHUNK, -1) + 1))
    nckq = -(-kend // KC)
    pairs = [(i, j) for i in range(nq) for j in range(-(-int(nckq[i]) // cpb))]
    qmap = jnp.asarray([a for a, _ in pairs], I32)
    kmap = jnp.asarray([b for _, b in pairs], I32)
    idx_bits = max(1, int(np.ceil(np.log2(Mp))))
    grid_spec = pltpu.PrefetchScalarGridSpec(
        num_scalar_prefetch=3,
        grid=(len(pairs),),
        in_specs=[
            pl.BlockSpec((tq, C), lambda p, qm, km, nk: (qm[p], 0)),
            pl.BlockSpec((tk, C), lambda p, qm, km, nk: (km[p], 1)),
            pl.BlockSpec((cpb, C, KC), lambda p, qm, km, nk: (km[p], 0, 0)),
            pl.BlockSpec((tq, h_idx * d_idx), lambda p, qm, km, nk: (qm[p], 0)),
            pl.BlockSpec((Mp, d_idx), lambda p, qm, km, nk: (0, 0)),
            pl.BlockSpec((LANES, tq), lambda p, qm, km, nk: (0, qm[p])),
        ],
        out_specs=pl.BlockSpec((tq, C), lambda p, qm, km, nk: (qm[p], 0)),
        scratch_shapes=[
            pltpu.VMEM((Mp // KC, KC, tq), I32),
            pltpu.VMEM((1, tq), I32), pltpu.VMEM((1, tq), I32),
            pltpu.VMEM((tq, C), BF), pltpu.VMEM((SUBLANES, tq), F32),
            pltpu.VMEM((n_heads, 1, tq), F32), pltpu.VMEM((n_heads, 1, tq), F32),
            pltpu.VMEM((n_heads, hd, tq), F32),
        ],
    )
    return pl.pallas_call(
        functools.partial(_dsa_prompt_kernel, n_heads=n_heads, hd=hd, h_idx=h_idx, d_idx=d_idx, n_meta=n_meta,
                          t_valid=t_valid, n_sel=n_sel, cpb=cpb, idx_bits=idx_bits),
        grid_spec=grid_spec,
        out_shape=jax.ShapeDtypeStruct((Mp, C), BF),
        compiler_params=_cparams(("arbitrary",)),
        name="dsa_prompt",
    )(qmap, kmap, jnp.asarray(nckq, I32), proj, proj, vt3, qi, ki, smt)


def _dsa_sample_kernel(q_ref, kn_ref, vn_ref, kc_ref, vc_ref, qi_ref, kin_ref, kic_ref, smt_ref, o_ref,
                       S_ref, thr_ref, cut_ref, qs_ref, w_ref, m_ref, l_ref, acc_ref,
                       *, n_heads, hd, h_idx, d_idx, past, n_sel, cpb, nkb, idx_bits):
    kb = pl.program_id(1)
    tq = q_ref.shape[0]
    ncache = past // KC
    nck = ncache + 1
    qpos = past + lax.broadcasted_iota(I32, (KC, tq), 1)

    @pl.when(kb == 0)
    def _():
        qs_ref[...] = (q_ref[...] * (hd ** -0.5)).astype(BF)
        w_ref[...] = smt_ref[d_idx:d_idx + SUBLANES, :] * (d_idx ** -0.5)

        def score_body(c, carry):
            k0 = pl.multiple_of(c * KC, KC)
            sc = _index_scores(kic_ref[pl.ds(k0, KC), :], qi_ref, w_ref[...], h_idx, d_idx)
            S_ref[c] = _sort_key(sc)
            return carry
        lax.fori_loop(0, ncache, score_body, 0)
        S_ref[ncache] = jnp.full((KC, tq), INT_MIN, I32)
        sc = _index_scores(kin_ref[...], qi_ref, w_ref[...], h_idx, d_idx)
        S_ref[ncache, 0:tq, :] = _sort_key(sc)
        _select_threshold(S_ref, nck, n_sel, tq, idx_bits, thr_ref, cut_ref)
        _flash_init(m_ref, l_ref, acc_ref)

    thr, cut = thr_ref[...], cut_ref[...]

    def chunk_body(c, carry):
        cg = kb * cpb + c
        rows = pl.ds(pl.multiple_of(c * KC, KC), KC)
        kpos = cg * KC + lax.broadcasted_iota(I32, (KC, tq), 0)
        dist = jnp.abs(qpos - kpos).astype(F32)
        vt = vc_ref[rows, :].T.astype(BF)
        _flash_update(S_ref[cg], cg * KC, kc_ref[rows, :].astype(BF), vt, qs_ref, dist,
                      thr, cut, m_ref, l_ref, acc_ref, n_heads, hd)
        return carry
    lax.fori_loop(0, cpb, chunk_body, 0)

    @pl.when(kb == nkb - 1)
    def _():
        tn = kn_ref.shape[0]
        kpos = past + lax.broadcasted_iota(I32, (tn, tq), 0)
        dist = jnp.abs(past + lax.broadcasted_iota(I32, (tn, tq), 1) - kpos).astype(F32)
        vn = vn_ref[...]
        if tn % LANES:
            vn = jnp.concatenate([vn, jnp.zeros((LANES - tn % LANES, vn.shape[1]), F32)], axis=0)
        vt = vn.T[:, :tn].astype(BF)
        _flash_update(S_ref[ncache, 0:tn, :], past, kn_ref[...].astype(BF), vt, qs_ref, dist,
                      thr, cut, m_ref, l_ref, acc_ref, n_heads, hd)
        _flash_finish(o_ref, l_ref, acc_ref, n_heads, hd, tq)


def dsa_sample(proj, kcache, vcache, qi, ki, kicache, smt, l, *, n_batch, t_new, n_heads, hd, h_idx, d_idx, n_sel, tk):
    C = n_heads * hd
    past = kcache.shape[2]
    cpb = tk // KC
    nkb = past // tk
    idx_bits = max(1, int(np.ceil(np.log2(past + KC))))
    tq = t_new
    return pl.pallas_call(
        functools.partial(_dsa_sample_kernel, n_heads=n_heads, hd=hd, h_idx=h_idx, d_idx=d_idx, past=past,
                          n_sel=n_sel, cpb=cpb, nkb=nkb, idx_bits=idx_bits),
        grid=(n_batch, nkb),
        in_specs=[
            pl.BlockSpec((tq, C), lambda b, j: (b, 0)),
            pl.BlockSpec((tq, C), lambda b, j: (b, 1)),
            pl.BlockSpec((tq, C), lambda b, j: (b, 2)),
            pl.BlockSpec((None, None, tk, C), lambda b, j: (l, b, j, 0)),
            pl.BlockSpec((None, None, tk, C), lambda b, j: (l, b, j, 0)),
            pl.BlockSpec((tq, h_idx * d_idx), lambda b, j: (b, 0)),
            pl.BlockSpec((tq, d_idx), lambda b, j: (b, 0)),
            pl.BlockSpec((None, None, past, d_idx), lambda b, j: (l, b, 0, 0)),
            pl.BlockSpec((None, LANES, tq), lambda b, j: (b, 0, 0)),
        ],
        out_specs=pl.BlockSpec((tq, C), lambda b, j: (b, 0)),
        out_shape=jax.ShapeDtypeStruct((n_batch * tq, C), BF),
        scratch_shapes=[
            pltpu.VMEM((past // KC + 1, KC, tq), I32),
            pltpu.VMEM((1, tq), I32), pltpu.VMEM((1, tq), I32),
            pltpu.VMEM((tq, C), BF), pltpu.VMEM((SUBLANES, tq), F32),
            pltpu.VMEM((n_heads, 1, tq), F32), pltpu.VMEM((n_heads, 1, tq), F32),
            pltpu.VMEM((n_heads, hd, tq), F32),
        ],
        compiler_params=_cparams(("parallel", "arbitrary")),
        name="dsa_sample",
    )(proj, proj, proj, kcache, vcache, qi, ki, kicache, smt)


def _split3(x):
    hi = x.astype(BF)
    r = x - hi.astype(F32)
    mid = r.astype(BF)
    lo = (r - mid.astype(F32)).astype(BF)
    return hi, mid, lo


def _gdn_kernel(x_ref, gb_ref, sm_ref, cp_ref, s0_ref, cw_ref, ng_ref, ob_ref, sn_ref, S_ref, cbuf_ref,
                *, n_heads, hd, t_valid, conv_w, b_off, a_off, nchunks):
    c = pl.program_id(1)
    C = x_ref.shape[0]
    cd = n_heads * hd
    halo = conv_w - 1

    @pl.when(c == 0)
    def _():
        S_ref[...] = s0_ref[...]
        cbuf_ref[SUBLANES - halo:SUBLANES, :] = cp_ref[...]

    cbuf_ref[SUBLANES:SUBLANES + C, :] = x_ref[...]
    conv = None
    for i in range(conv_w):
        term = cbuf_ref[SUBLANES - halo + i:SUBLANES - halo + i + C, :] * cw_ref[i:i + 1, :]
        conv = term if conv is None else conv + term
    cbuf_ref[0:SUBLANES, :] = cbuf_ref[C:C + SUBLANES, :]
    act = conv * _sigmoid(conv)

    sm = sm_ref[...]
    row = c * C + lax.broadcasted_iota(I32, (C, LANES), 0)
    lane = lax.broadcasted_iota(I32, (C, LANES), 1)
    valid = row < t_valid
    la = jnp.where(valid & (lane >= a_off) & (lane < a_off + n_heads), sm, 0.0)
    beta_t = jnp.where(valid, sm, 0.0)
    ri = lax.broadcasted_iota(I32, (C, C), 0)
    ci = lax.broadcasted_iota(I32, (C, C), 1)
    causal = ri >= ci
    strict = ri > ci
    ltri = jnp.where(causal, 1.0, 0.0).astype(BF)
    g_t = sum(jnp.dot(ltri, piece, preferred_element_type=F32) for piece in _split3(la))
    g_rows = jnp.concatenate([g_t, jnp.zeros((LANES - C, LANES), F32)], axis=0).T

    for h in range(n_heads):
        cols = slice(h * hd, (h + 1) * hd)
        qh = act[:, h * hd:(h + 1) * hd]
        kh = act[:, cd + h * hd:cd + (h + 1) * hd]
        vh = act[:, 2 * cd + h * hd:2 * cd + (h + 1) * hd]
        qn = qh * lax.rsqrt(jnp.sum(qh * qh, axis=-1, keepdims=True) + NORM_EPS) * (hd ** -0.5)
        kn = kh * lax.rsqrt(jnp.sum(kh * kh, axis=-1, keepdims=True) + NORM_EPS)
        beta = beta_t[:, b_off + h:b_off + h + 1]
        g_c = g_t[:, a_off + h:a_off + h + 1]
        g_r = g_rows[a_off + h:a_off + h + 1, :C]
        g_last = g_t[C - 1:C, a_off + h:a_off + h + 1]
        decay = jnp.where(causal, jnp.exp(jnp.where(causal, g_c - g_r, 0.0)), 0.0)
        kb = kn.astype(BF)
        kk = lax.dot_general(kb, kb, NT_DIMS, preferred_element_type=F32)
        X = jnp.where(strict, -(beta * kk * decay), 0.0)
        N, P = X, X
        for _ in range(int(np.log2(C)) - 1):
            Pb = P.astype(BF)
            P = jnp.dot(Pb, Pb, preferred_element_type=F32)
            N = N + P + jnp.dot(N.astype(BF), P.astype(BF), preferred_element_type=F32)
        eg = jnp.exp(g_c)
        rhs = jnp.concatenate([vh * beta, kn * (beta * eg)], axis=1)
        sol = rhs + jnp.dot(N.astype(BF), rhs.astype(BF), preferred_element_type=F32)
        Sh = S_ref[h]
        Sb = Sh.astype(BF)
        u = sol[:, :hd] - jnp.dot(sol[:, hd:].astype(BF), Sb, preferred_element_type=F32)
        ub = u.astype(BF)
        qk = lax.dot_general(qn.astype(BF), kb, NT_DIMS, preferred_element_type=F32) * decay
        o = (jnp.dot((qn * eg).astype(BF), Sb, preferred_element_type=F32)
             + jnp.dot(qk.astype(BF), ub, preferred_element_type=F32))
        kd = (kn * jnp.exp(g_last - g_c)).astype(BF)
        S_ref[h] = Sh * jnp.exp(g_last) + lax.dot_general(kd, ub, TN_DIMS, preferred_element_type=F32)
        gh = gb_ref[:, cols]
        on = o * lax.rsqrt(jnp.mean(o * o, axis=-1, keepdims=True) + NORM_EPS) * ng_ref[...]
        ob_ref[:, cols] = (on * (gh * _sigmoid(gh))).astype(BF)

    @pl.when(c == nchunks - 1)
    def _():
        sn_ref[...] = S_ref[...]


def gdn(proj, sm, conv_prev, s0, cw, ng, *, n_batch, rows_per_batch, t_valid, n_heads, hd, col0, b_off, a_off):
    cd = n_heads * hd
    nchunks = rows_per_batch // DN_C
    conv_w = cw.shape[0]
    qkv_blk = col0 // (3 * cd)
    g_blk = (col0 + 3 * cd) // cd
    return pl.pallas_call(
        functools.partial(_gdn_kernel, n_heads=n_heads, hd=hd, t_valid=t_valid, conv_w=conv_w,
                          b_off=b_off, a_off=a_off, nchunks=nchunks),
        grid=(n_batch, nchunks),
        in_specs=[
            pl.BlockSpec((DN_C, 3 * cd), lambda b, c: (b * nchunks + c, qkv_blk)),
            pl.BlockSpec((DN_C, cd), lambda b, c: (b * nchunks + c, g_blk)),
            pl.BlockSpec((DN_C, LANES), lambda b, c: (b * nchunks + c, 0)),
            pl.BlockSpec((None, conv_w - 1, 3 * cd), lambda b, c: (b, 0, 0)),
            pl.BlockSpec((None, n_heads, hd, hd), lambda b, c: (b, 0, 0, 0)),
            pl.BlockSpec((conv_w, 3 * cd), lambda b, c: (0, 0)),
            pl.BlockSpec((1, hd), lambda b, c: (0, 0)),
        ],
        out_specs=[pl.BlockSpec((DN_C, cd), lambda b, c: (b * nchunks + c, 0)),
                   pl.BlockSpec((None, n_heads, hd, hd), lambda b, c: (b, 0, 0, 0))],
        out_shape=[jax.ShapeDtypeStruct((n_batch * rows_per_batch, cd), BF),
                   jax.ShapeDtypeStruct((n_batch, n_heads, hd, hd), F32)],
        scratch_shapes=[pltpu.VMEM((n_heads, hd, hd), F32),
                        pltpu.VMEM((DN_C + SUBLANES, 3 * cd), F32)],
        compiler_params=_cparams(("parallel", "arbitrary")),
        name="gdn",
    )(proj, proj, sm, conv_prev, s0, cw, ng)


def _merge_kernel(h_ref, hb_ref, oa_ref, ob_ref, wga_ref, wgb_ref, woa_ref, wob_ref, wout_ref, g_ref, b_ref,
                  o_ref, acc_ref, *, alpha, nn, rs):
    j = pl.program_id(1)

    @pl.when(j == 0)
    def _():
        acc_ref[...] = jnp.zeros_like(acc_ref)

    hb = hb_ref[...]
    ga = _sigmoid(jnp.dot(hb, wga_ref[...], preferred_element_type=F32))
    gb = _sigmoid(jnp.dot(hb, wgb_ref[...], preferred_element_type=F32))
    y = (ga * jnp.dot(oa_ref[...], woa_ref[...], preferred_element_type=F32)
         + gb * jnp.dot(ob_ref[...], wob_ref[...], preferred_element_type=F32))
    acc_ref[...] += jnp.dot(y.astype(BF), wout_ref[...], preferred_element_type=F32)

    @pl.when(j == nn - 1)
    def _():
        def slab(r, carry):
            rows = pl.ds(pl.multiple_of(r * rs, rs), rs)
            o_ref[rows, :] = _layer_norm(alpha * h_ref[rows, :] + acc_ref[rows, :], g_ref[...], b_ref[...])
            return carry
        lax.fori_loop(0, h_ref.shape[0] // rs, slab, 0)


def merge_ln(h, hb, oa, ob, wgate, woa, wob, wout, g, b, l, alpha, tm, tn):
    M, D = h.shape
    ca, cb = oa.shape[1], ob.shape[1]
    nn = D // tn
    rs = 64 if tm % 64 == 0 else 16
    return pl.pallas_call(
        functools.partial(_merge_kernel, alpha=alpha, nn=nn, rs=rs),
        grid=(M // tm, nn),
        in_specs=[
            pl.BlockSpec((tm, D), lambda i, j: (i, 0)),
            pl.BlockSpec((tm, D), lambda i, j: (i, 0)),
            pl.BlockSpec((tm, ca), lambda i, j: (i, 0)),
            pl.BlockSpec((tm, cb), lambda i, j: (i, 0)),
            pl.BlockSpec((None, None, D, tn), lambda i, j: (l, 0, 0, j)),
            pl.BlockSpec((None, None, D, tn), lambda i, j: (l, 1, 0, j)),
            pl.BlockSpec((None, ca, tn), lambda i, j: (l, 0, j)),
            pl.BlockSpec((None, cb, tn), lambda i, j: (l, 0, j)),
            pl.BlockSpec((None, tn, D), lambda i, j: (l, j, 0)),
            pl.BlockSpec((1, D), lambda i, j: (0, 0)),
            pl.BlockSpec((1, D), lambda i, j: (0, 0)),
        ],
        out_specs=pl.BlockSpec((tm, D), lambda i, j: (i, 0)),
        out_shape=jax.ShapeDtypeStruct((M, D), F32),
        scratch_shapes=[pltpu.VMEM((tm, D), F32)],
        compiler_params=_cparams(("parallel", "arbitrary")),
        name="merge_ln",
    )(h, hb, oa, ob, wgate, wgate, woa, wob, wout, g, b)


def _round_up(n, m):
    return -(-n // m) * m


def kernel(x_prompt, x_sample, cache_k, cache_v, cache_kidx, state_conv, state_dn, meta, w_ff_gu, w_ff_down, ln_g, ln_b, w_in, conv_w, idx_ln_g, idx_ln_b, a_log, dt_bias, dn_norm_g, w_oa, w_ob, w_out):
    depth = w_in.shape[0]
    B, S, D = x_prompt.shape
    Bs, Ts, _ = x_sample.shape
    n_meta = meta.shape[0]
    past, HA, HDA = cache_k.shape[2], cache_k.shape[3], cache_k.shape[4]
    CA = HA * HDA
    d_idx = cache_kidx.shape[-1]
    HB, HDB = state_dn.shape[2], state_dn.shape[3]
    CD = HB * HDB
    cw_len = conv_w.shape[1]
    F = w_ff_down.shape[2]
    n_in = w_in.shape[2]
    h_idx = (n_in - 3 * CA - d_idx - 4 * CD - 2 * HB - 2 * D) // (d_idx + 1)
    assert B == 1 and Ts == DN_C and CHUNK == 64 and n_meta <= CHUNK
    assert d_idx + h_idx + 2 * HB <= LANES and h_idx <= SUBLANES and HA * HDA == CA
    assert past % KC == 0 and CA == CD
    alpha = float((2 * depth) ** 0.25)
    T = n_meta + S
    Mp = _round_up(T, ROW_PAD)
    n_sel_p = min(MAX_SEL, S // 4)
    n_sel_s = min(MAX_SEL, (past + Ts) // 4)
    w_off, b_off, a_off = d_idx, d_idx + h_idx, d_idx + h_idx + HB

    Fp = _round_up(F, FF_ALIGN)
    wgu = w_ff_gu.reshape(depth, 2, D, 2, F).transpose(0, 1, 3, 2, 4)
    wgu = jnp.pad(wgu, ((0, 0),) * 4 + ((0, Fp - F),)).astype(BF)
    wd = jnp.pad(w_ff_down, ((0, 0), (0, 0), (0, Fp - F), (0, 0))).astype(BF)
    offs = np.cumsum([0, CA, CA, CA, h_idx * d_idx, d_idx, h_idx, 3 * CD, CD, HB, HB, D, D])
    col = lambda i: w_in[:, :, offs[i]:offs[i + 1]]
    w_main = jnp.concatenate([col(0), col(1), col(2), col(6), col(7)], axis=2).astype(BF)
    pad_small = LANES - (d_idx + h_idx + 2 * HB)
    w_idx = jnp.concatenate([col(3), col(4), col(5), col(8), col(9),
                             jnp.zeros((depth, D, pad_small), F32)], axis=2).astype(BF)
    w_gate = jnp.stack([col(10), col(11)], axis=1).astype(BF)
    woa, wob, wout = w_oa.astype(BF), w_ob.astype(BF), w_out.astype(BF)

    def lane_vec(v, off):
        return jnp.zeros((depth, 1, LANES), F32).at[:, 0, off:off + v.shape[1]].set(v)
    lng_v, lnb_v = lane_vec(idx_ln_g, 0), lane_vec(idx_ln_b, 0)
    alog_v, dtb_v = lane_vec(a_log, a_off), lane_vec(dt_bias, a_off)

    xp = jnp.concatenate([meta.astype(F32), x_prompt[0], jnp.zeros((Mp - T, D), F32)], axis=0)
    xs = x_sample.reshape(Bs * Ts, D)
    Ms = Bs * Ts
    tm_p, tm_s = _pick_tile(Mp, TM_FFN), _pick_tile(Ms, 512)
    tmm_p, tmm_s = _pick_tile(Mp, TM_MM), _pick_tile(Ms, 1024)
    tf = FF_ALIGN
    tn_mm = _pick_tile(w_main.shape[2], TN_MM)
    tn_mg = _pick_tile(D, TN_MERGE)
    tq = _pick_tile(Mp, TQ)
    tk_p = _pick_tile(Mp, TK_PROMPT)
    tk_c = _pick_tile(past, TK_CACHE)
    assert tq % LANES == 0 and tk_p % KC == 0 and tk_c % KC == 0 and Mp % KC == 0

    kcache = cache_k.reshape(depth, Bs, past, CA)
    vcache = cache_v.reshape(depth, Bs, past, CA)
    kicache = cache_kidx.astype(BF)
    conv0 = jnp.zeros((1, cw_len - 1, 3 * CD), F32)
    s0 = jnp.zeros((1, HB, HDB, HDB), F32)

    outs = [[] for _ in range(10)]
    for l in range(depth):
        g3 = [ln_g[l, i][None] for i in range(3)]
        b3 = [ln_b[l, i][None] for i in range(3)]
        ng = dn_norm_g[l][None]
        streams = []
        for (x, is_prompt) in ((xp, True), (xs, False)):
            tm, tmm = (tm_p, tmm_p) if is_prompt else (tm_s, tmm_s)
            h, hb = ffn_ln(x, wgu, wd, g3[0], b3[0], l, 0, alpha, tm, tf)
            proj = matmul(hb, w_main, l, tmm, tn_mm)
            qi, ki, sm = idx_proj(hb, w_idx, lng_v[l], lnb_v[l], alog_v[l], dtb_v[l], l, tmm, d_idx, h_idx, HB)
            if is_prompt:
                vt3 = proj[:, 2 * CA:3 * CA].reshape(Mp // KC, KC, CA).swapaxes(1, 2)
                oa = dsa_prompt(proj, vt3, qi, ki, sm.T, t_valid=T, n_meta=n_meta, n_heads=HA, hd=HDA,
                                h_idx=h_idx, d_idx=d_idx, n_sel=n_sel_p, tq=tq, tk=tk_p)
                ob, s_new = gdn(proj, sm, conv0, s0, conv_w[l], ng, n_batch=1, rows_per_batch=Mp, t_valid=T,
                                n_heads=HB, hd=HDB, col0=3 * CA, b_off=b_off, a_off=a_off)
                nrow = T
            else:
                smt = sm.reshape(Bs, Ts, LANES).swapaxes(1, 2)
                oa = dsa_sample(proj, kcache, vcache, qi, ki, kicache, smt, l, n_batch=Bs, t_new=Ts,
                                n_heads=HA, hd=HDA, h_idx=h_idx, d_idx=d_idx, n_sel=n_sel_s, tk=tk_c)
                ob, s_new = gdn(proj, sm, state_conv[l], state_dn[l], conv_w[l], ng, n_batch=Bs, rows_per_batch=Ts,
                                t_valid=Ts, n_heads=HB, hd=HDB, col0=3 * CA, b_off=b_off, a_off=a_off)
                nrow = Ms
            x = merge_ln(h, hb, oa, ob, w_gate, woa, wob, wout, g3[1], b3[1], l, alpha, tm, tn_mg)
            x, _ = ffn_ln(x, wgu, wd, g3[2], b3[2], l, 1, alpha, tm, tf)
            nb = 1 if is_prompt else Bs
            per = nrow // nb
            k_new = proj[:nrow, CA:2 * CA].reshape(nb, per, HA, HDA)
            v_new = proj[:nrow, 2 * CA:3 * CA].reshape(nb, per, HA, HDA)
            ki_new = sm[:nrow, :d_idx].reshape(nb, per, d_idx)
            qkv_b = proj[:nrow, 3 * CA:3 * CA + 3 * CD].reshape(nb, per, 3 * CD)
            if is_prompt:
                conv_new = qkv_b[:, per - (cw_len - 1):]
            else:
                conv_new = jnp.concatenate([state_conv[l], qkv_b], axis=1)[:, -(cw_len - 1):]
            streams.append((x, k_new, v_new, ki_new, conv_new, s_new))
        (xp, *st_p), (xs, *st_s) = streams
        for i, a in enumerate(st_p):
            outs[i].append(a)
        for i, a in enumerate(st_s):
            outs[5 + i].append(a)
    y_prompt = xp[n_meta:T][None]
    y_sample = xs.reshape(Bs, Ts, D)
    return (y_prompt, y_sample) + tuple(jnp.stack(o) for o in outs)
```

```python
import functools

import numpy as np
import jax
import jax.numpy as jnp
from jax import lax
from jax.experimental import pallas as pl
from jax.experimental.pallas import tpu as pltpu

CHUNK = 64
MAX_SEL = 256
DN_C = 64
LN_EPS = 1e-5
NORM_EPS = 1e-6

LANES = 128
SUBLANES = 8
KC = 256
FF_ALIGN = 512
VMEM_LIMIT = 56 << 20

ROW_PAD = 1280
TM_FFN = 640
TM_MM = 1280
TN_MM = 1024
TN_MERGE = 512
TQ = 256
TK_PROMPT = 1280
TK_CACHE = 1024

BF = jnp.bfloat16
F32 = jnp.float32
I32 = jnp.int32
INT_MIN = -(2 ** 31)
NEG = -1e30
LOG2E = 1.4426950408889634
ONES_ROWS = 16
NT_DIMS = (((1,), (1,)), ((), ()))
TN_DIMS = (((0,), (0,)), ((), ()))


def _cparams(sem):
    return pltpu.CompilerParams(dimension_semantics=sem, vmem_limit_bytes=VMEM_LIMIT)


def _sigmoid(x):
    return 1.0 / (1.0 + jnp.exp(-x))


def _layer_norm(y, g, b):
    mu = jnp.mean(y, axis=-1, keepdims=True)
    d = y - mu
    var = jnp.mean(d * d, axis=-1, keepdims=True)
    return d * lax.rsqrt(var + LN_EPS) * g + b


def _pick_tile(n, pref):
    if n <= pref:
        return n
    for t in range(pref, 15, -16):
        if n % t == 0:
            return t
    return n


def _ffn_kernel(x_ref, wg_ref, wu_ref, wd_ref, g_ref, b_ref, o_ref, ob_ref, xb_ref, acc_ref, *, alpha, nf, rs):
    j = pl.program_id(1)

    @pl.when(j == 0)
    def _():
        xb_ref[...] = x_ref[...].astype(BF)
        acc_ref[...] = jnp.zeros_like(acc_ref)

    xb = xb_ref[...]
    hg = jnp.dot(xb, wg_ref[...], preferred_element_type=F32)
    hu = jnp.dot(xb, wu_ref[...], preferred_element_type=F32)
    a = (hg * _sigmoid(hg) * hu).astype(BF)
    acc_ref[...] += jnp.dot(a, wd_ref[...], preferred_element_type=F32)

    @pl.when(j == nf - 1)
    def _():
        def slab(r, carry):
            rows = pl.ds(pl.multiple_of(r * rs, rs), rs)
            y = alpha * x_ref[rows, :] + 0.5 * acc_ref[rows, :]
            z = _layer_norm(y, g_ref[...], b_ref[...])
            o_ref[rows, :] = z
            ob_ref[rows, :] = z.astype(BF)
            return carry
        lax.fori_loop(0, x_ref.shape[0] // rs, slab, 0)


def ffn_ln(x, wg, wu, wd, g, b, l, s, alpha, tm, tf):
    M, D = x.shape
    nf = wd.shape[2] // tf
    rs = 64 if tm % 64 == 0 else 16
    return pl.pallas_call(
        functools.partial(_ffn_kernel, alpha=alpha, nf=nf, rs=rs),
        grid=(M // tm, nf),
        in_specs=[
            pl.BlockSpec((tm, D), lambda i, j: (i, 0)),
            pl.BlockSpec((None, None, D, tf), lambda i, j: (l, s, 0, j)),
            pl.BlockSpec((None, None, D, tf), lambda i, j: (l, s, 0, j)),
            pl.BlockSpec((None, None, tf, D), lambda i, j: (l, s, j, 0)),
            pl.BlockSpec((1, D), lambda i, j: (0, 0)),
            pl.BlockSpec((1, D), lambda i, j: (0, 0)),
        ],
        out_specs=[pl.BlockSpec((tm, D), lambda i, j: (i, 0)), pl.BlockSpec((tm, D), lambda i, j: (i, 0))],
        out_shape=[jax.ShapeDtypeStruct((M, D), F32), jax.ShapeDtypeStruct((M, D), BF)],
        scratch_shapes=[pltpu.VMEM((tm, D), BF), pltpu.VMEM((tm, D), F32)],
        compiler_params=_cparams(("parallel", "arbitrary")),
        name="ffn_ln",
    )(x, wg, wu, wd, g, b)


def _mm_kernel(x_ref, w_ref, o_ref):
    o_ref[...] = jnp.dot(x_ref[...], w_ref[...], preferred_element_type=F32)


def matmul(xb, w, l, tm, tn):
    M, D = xb.shape
    N = w.shape[2]
    return pl.pallas_call(
        _mm_kernel,
        grid=(M // tm, N // tn),
        in_specs=[pl.BlockSpec((tm, D), lambda i, j: (i, 0)),
                  pl.BlockSpec((None, D, tn), lambda i, j: (l, 0, j))],
        out_specs=pl.BlockSpec((tm, tn), lambda i, j: (i, j)),
        out_shape=jax.ShapeDtypeStruct((M, N), F32),
        compiler_params=_cparams(("parallel", "arbitrary")),
        name="in_proj",
    )(xb, w)


def _idx_kernel(x_ref, w_ref, lng_ref, lnb_ref, alog_ref, dtb_ref, qi_ref, ki_ref, sm_ref, *, n_qi, d_idx, h_idx, h_b):
    r = jnp.dot(x_ref[...], w_ref[...], preferred_element_type=F32)
    qi_ref[...] = r[:, :n_qi].astype(BF)
    t = r[:, n_qi:]
    lane = lax.broadcasted_iota(I32, t.shape, 1)
    isk = lane < d_idx
    mu = jnp.sum(jnp.where(isk, t, 0.0), axis=-1, keepdims=True) / d_idx
    d = jnp.where(isk, t - mu, 0.0)
    var = jnp.sum(d * d, axis=-1, keepdims=True) / d_idx
    kn = d * lax.rsqrt(var + LN_EPS) * lng_ref[...] + lnb_ref[...]
    wv = t * (h_idx ** -0.5)
    beta = _sigmoid(t)
    z = t + dtb_ref[...]
    softplus = jnp.maximum(z, 0.0) + jnp.log(1.0 + jnp.exp(-jnp.abs(z)))
    la = -jnp.exp(alog_ref[...]) * softplus
    w_off, b_off, a_off = d_idx, d_idx + h_idx, d_idx + h_idx + h_b
    out = jnp.where(isk, kn,
                    jnp.where(lane < b_off, wv,
                              jnp.where(lane < a_off, beta,
                                        jnp.where(lane < a_off + h_b, la, 0.0))))
    sm_ref[...] = out
    ki_ref[...] = kn[:, :d_idx].astype(BF)


def idx_proj(xb, w, lng, lnb, alog, dtb, l, tm, d_idx, h_idx, h_b):
    M, D = xb.shape
    N = w.shape[2]
    n_qi = N - LANES
    vec = pl.BlockSpec((1, LANES), lambda i: (0, 0))
    return pl.pallas_call(
        functools.partial(_idx_kernel, n_qi=n_qi, d_idx=d_idx, h_idx=h_idx, h_b=h_b),
        grid=(M // tm,),
        in_specs=[pl.BlockSpec((tm, D), lambda i: (i, 0)),
                  pl.BlockSpec((None, D, N), lambda i: (l, 0, 0)),
                  vec, vec, vec, vec],
        out_specs=[pl.BlockSpec((tm, n_qi), lambda i: (i, 0)),
                   pl.BlockSpec((tm, d_idx), lambda i: (i, 0)),
                   pl.BlockSpec((tm, LANES), lambda i: (i, 0))],
        out_shape=[jax.ShapeDtypeStruct((M, n_qi), BF),
                   jax.ShapeDtypeStruct((M, d_idx), BF),
                   jax.ShapeDtypeStruct((M, LANES), F32)],
        compiler_params=_cparams(("parallel",)),
        name="idx_proj",
    )(xb, w, lng, lnb, alog, dtb)


def _sort_key(s):
    b = pltpu.bitcast(s + 0.0, I32)
    return b ^ ((b >> 31) & 0x7FFFFFFF)


def _index_scores(ki_chunk, qi_ref, w_rows, h_idx, d_idx):
    s = [lax.dot_general(ki_chunk, qi_ref[:, h * d_idx:(h + 1) * d_idx], NT_DIMS, preferred_element_type=F32)
         for h in range(h_idx)]
    acc = w_rows[0:1, :] * jnp.maximum(s[0], 0.0)
    for h in range(1, h_idx):
        acc = acc + w_rows[h:h + 1, :] * jnp.maximum(s[h], 0.0)
    return acc


def _count_chunks(S_ref, nck, tq, pred):
    def body(c, cnt):
        s = S_ref[c]
        ones = jnp.where(pred(s, c), 1, 0).astype(I32)
        return cnt + ones.reshape(s.shape[0] // SUBLANES, SUBLANES, tq).sum(axis=0)
    cnt8 = lax.fori_loop(0, nck, body, jnp.zeros((SUBLANES, tq), I32))
    return cnt8.sum(axis=0, keepdims=True)


def _select_top(S_ref, nck, n_sel, tq, idx_bits, cut_ref):
    def bit_body(i, carry):
        thr, cnt_thr = carry
        cand = thr + jnp.left_shift(jnp.int32(1), 31 - i)
        cnt = _count_chunks(S_ref, nck, tq, lambda s, c: s >= cand)
        ok = cnt >= n_sel
        return jnp.where(ok, cand, thr), jnp.where(ok, cnt, cnt_thr)

    thr0 = jnp.full((1, tq), INT_MIN, I32)
    thr, cnt_thr = lax.fori_loop(0, 32, bit_body, (thr0, jnp.zeros((1, tq), I32)))
    has_tie = (thr > INT_MIN) & (cnt_thr > n_sel)
    cut_ref[...] = jnp.full((1, tq), 2 ** 30, I32)
    kc = S_ref.shape[1]

    @pl.when(jnp.max(has_tie.astype(I32)) > 0)
    def _():
        need = n_sel - _count_chunks(S_ref, nck, tq, lambda s, c: s > thr)

        def idx_body(i, cut):
            cand = cut + jnp.left_shift(jnp.int32(1), idx_bits - 1 - i)

            def pred(s, c):
                kidx = c * kc + lax.broadcasted_iota(I32, s.shape, 0)
                return (s == thr) & (kidx < cand)
            cnt = _count_chunks(S_ref, nck, tq, pred)
            return jnp.where(cnt < need, cand, cut)

        cut = lax.fori_loop(0, idx_bits, idx_body, jnp.zeros((1, tq), I32))
        cut_ref[...] = jnp.where(has_tie, cut, 2 ** 30)

    thr_sel, cut = jnp.maximum(thr, INT_MIN + 1), cut_ref[...]

    def bias_body(c, carry):
        keys = S_ref[c]
        kidx = c * kc + lax.broadcasted_iota(I32, keys.shape, 0)
        sel = (keys > thr_sel) | ((keys == thr_sel) & (kidx <= cut))
        S_ref[c] = pltpu.bitcast(jnp.where(sel, 0.0, NEG).astype(F32), I32)
        return carry
    lax.fori_loop(0, nck, bias_body, 0)


def _flash_update(bias, sdist, k_chunk, vt_chunk, qt_ref, m_ref, l_ref, acc_ref, lt_ref, p_ref, a_ref, n_heads, hd):
    kc = bias.shape[0]
    vrows = vt_chunk.shape[0] // n_heads
    for h in range(n_heads):
        cols = slice(h * hd, (h + 1) * hd)
        lt_ref[h, 0:kc, :] = jnp.dot(k_chunk[:, cols], qt_ref[cols, :], preferred_element_type=F32)
    for h in range(n_heads):
        slope = float(2.0 ** (-(8.0 / n_heads) * (h + 1)))
        lt = lt_ref[h, 0:kc, :] + (bias - slope * sdist)
        m_old = m_ref[h]
        m_new = jnp.maximum(m_old, jnp.max(lt, axis=0, keepdims=True))
        p = jnp.exp2(lt - m_new)
        a = jnp.exp2(m_old - m_new)
        if vrows == hd:
            l_ref[h] = a * l_ref[h] + jnp.sum(p, axis=0, keepdims=True)
        p_ref[h, 0:kc, :] = p.astype(BF)
        a_ref[h] = a
        m_ref[h] = m_new
    for h in range(n_heads):
        pv = jnp.dot(vt_chunk[h * vrows:(h + 1) * vrows, :], p_ref[h, 0:kc, :], preferred_element_type=F32)
        a = a_ref[h]
        acc_ref[h] = a * acc_ref[h] + pv[:hd]
        if vrows > hd:
            l_ref[h] = a * l_ref[h] + pv[hd:hd + 1]


def _flash_init(m_ref, l_ref, acc_ref):
    m_ref[...] = jnp.full(m_ref.shape, NEG, F32)
    l_ref[...] = jnp.zeros(l_ref.shape, F32)
    acc_ref[...] = jnp.zeros(acc_ref.shape, F32)


def _flash_finish(o_ref, l_ref, acc_ref, n_heads, hd, tq):
    for h in range(n_heads):
        ot = acc_ref[h] / l_ref[h]
        if tq % LANES:
            ot = jnp.concatenate([ot, jnp.zeros((hd, LANES - tq % LANES), F32)], axis=1)
        o_ref[:, h * hd:(h + 1) * hd] = ot.T[:tq].astype(BF)


def _dsa_prompt_kernel(qmap, kmap, nckq, q_ref, k_ref, vt_ref, qi_ref, ki_ref, smt_ref, o_ref,
                       S_ref, cut_ref, qt_ref, w_ref, m_ref, l_ref, acc_ref, lt_ref, p_ref, a_ref,
                       *, n_heads, hd, h_idx, d_idx, n_meta, t_valid, n_sel, cpb, idx_bits):
    p = pl.program_id(0)
    qb, kb = qmap[p], kmap[p]
    nck = nckq[qb]
    tq = q_ref.shape[0]
    qpos = qb * tq + lax.broadcasted_iota(I32, (KC, tq), 1)

    @pl.when(kb == 0)
    def _():
        qt_ref[...] = (q_ref[...] * (hd ** -0.5 * LOG2E)).T.astype(BF)
        w_ref[...] = smt_ref[d_idx:d_idx + SUBLANES, :] * (d_idx ** -0.5)
        qchk = jnp.maximum((qpos - n_meta) >> 6, -1)

        def score_body(c, carry):
            k0 = pl.multiple_of(c * KC, KC)
            sc = _index_scores(ki_ref[pl.ds(k0, KC), :], qi_ref, w_ref[...], h_idx, d_idx)
            kpos = k0 + lax.broadcasted_iota(I32, (KC, tq), 0)
            allowed = (jnp.maximum((kpos - n_meta) >> 6, -1) <= qchk) & (kpos < t_valid)
            S_ref[c] = jnp.where(allowed, _sort_key(sc), INT_MIN)
            return carry
        lax.fori_loop(0, nck, score_body, 0)
        _select_top(S_ref, nck, n_sel, tq, idx_bits, cut_ref)
        _flash_init(m_ref, l_ref, acc_ref)

    def chunk_body(c, carry):
        cg = kb * cpb + c
        rows = pl.ds(pl.multiple_of(c * KC, KC), KC)
        kpos = cg * KC + lax.broadcasted_iota(I32, (KC, tq), 0)
        sdist = jnp.abs(qpos - kpos).astype(F32) * LOG2E
        _flash_update(pltpu.bitcast(S_ref[cg], F32), sdist, k_ref[rows, :], vt_ref[c], qt_ref,
                      m_ref, l_ref, acc_ref, lt_ref, p_ref, a_ref, n_heads, hd)
        return carry
    lax.fori_loop(0, jnp.minimum(cpb, nck - kb * cpb), chunk_body, 0)

    @pl.when((kb + 1) * cpb >= nck)
    def _():
        _flash_finish(o_ref, l_ref, acc_ref, n_heads, hd, tq)


def dsa_prompt(proj, kb16, vt3, qi, ki, smt, *, t_valid, n_meta, n_heads, hd, h_idx, d_idx, n_sel, tq, tk):
    Mp = proj.shape[0]
    C = n_heads * hd
    vr = vt3.shape[1]
    cpb = tk // KC
    nq = Mp // tq
    qlast = np.minimum(np.arange(nq) * tq + tq - 1, t_valid - 1)
    kend = np.minimum(t_valid, n_meta + CHUNK * (np.maximum((qlast - n_meta) // CHUNK, -1) + 1))
    nckq = -(-kend // KC)
    pairs = [(i, j) for i in range(nq) for j in range(-(-int(nckq[i]) // cpb))]
    qmap = jnp.asarray([a for a, _ in pairs], I32)
    kmap = jnp.asarray([b for _, b in pairs], I32)
    idx_bits = max(1, int(np.ceil(np.log2(Mp))))
    grid_spec = pltpu.PrefetchScalarGridSpec(
        num_scalar_prefetch=3,
        grid=(len(pairs),),
        in_specs=[
            pl.BlockSpec((tq, C), lambda p, qm, km, nk: (qm[p], 0)),
            pl.BlockSpec((tk, C), lambda p, qm, km, nk: (km[p], 0)),
            pl.BlockSpec((cpb, vr, KC), lambda p, qm, km, nk: (km[p], 0, 0)),
            pl.BlockSpec((tq, h_idx * d_idx), lambda p, qm, km, nk: (qm[p], 0)),
            pl.BlockSpec((Mp, d_idx), lambda p, qm, km, nk: (0, 0)),
            pl.BlockSpec((LANES, tq), lambda p, qm, km, nk: (0, qm[p])),
        ],
        out_specs=pl.BlockSpec((tq, C), lambda p, qm, km, nk: (qm[p], 0)),
        scratch_shapes=[
            pltpu.VMEM((Mp // KC, KC, tq), I32),
            pltpu.VMEM((1, tq), I32),
            pltpu.VMEM((C, tq), BF), pltpu.VMEM((SUBLANES, tq), F32),
            pltpu.VMEM((n_heads, 1, tq), F32), pltpu.VMEM((n_heads, 1, tq), F32),
            pltpu.VMEM((n_heads, hd, tq), F32),
            pltpu.VMEM((n_heads, KC, tq), F32), pltpu.VMEM((n_heads, KC, tq), BF),
            pltpu.VMEM((n_heads, 1, tq), F32),
        ],
    )
    return pl.pallas_call(
        functools.partial(_dsa_prompt_kernel, n_heads=n_heads, hd=hd, h_idx=h_idx, d_idx=d_idx, n_meta=n_meta,
                          t_valid=t_valid, n_sel=n_sel, cpb=cpb, idx_bits=idx_bits),
        grid_spec=grid_spec,
        out_shape=jax.ShapeDtypeStruct((Mp, C), BF),
        compiler_params=_cparams(("arbitrary",)),
        name="dsa_prompt",
    )(qmap, kmap, jnp.asarray(nckq, I32), proj, kb16, vt3, qi, ki, smt)


def _dsa_sample_kernel(q_ref, kn_ref, vn_ref, kc_ref, vc_ref, qi_ref, kin_ref, kic_ref, smt_ref, o_ref,
                       S_ref, cut_ref, qt_ref, w_ref, m_ref, l_ref, acc_ref, lt_ref, p_ref, a_ref,
                       *, n_heads, hd, h_idx, d_idx, past, n_sel, cpb, nkb, idx_bits):
    kb = pl.program_id(1)
    tq = q_ref.shape[0]
    ncache = past // KC
    nck = ncache + 1
    qpos = past + lax.broadcasted_iota(I32, (KC, tq), 1)

    @pl.when(kb == 0)
    def _():
        qs = q_ref[...] * (hd ** -0.5 * LOG2E)
        if tq % LANES:
            qs = jnp.concatenate([qs, jnp.zeros((LANES - tq % LANES, qs.shape[1]), F32)], axis=0)
        qt_ref[...] = qs.T[:, :tq].astype(BF)
        w_ref[...] = smt_ref[d_idx:d_idx + SUBLANES, :] * (d_idx ** -0.5)

        def score_body(c, carry):
            k0 = pl.multiple_of(c * KC, KC)
            sc = _index_scores(kic_ref[pl.ds(k0, KC), :], qi_ref, w_ref[...], h_idx, d_idx)
            S_ref[c] = _sort_key(sc)
            return carry
        lax.fori_loop(0, ncache, score_body, 0)
        S_ref[ncache] = jnp.full((KC, tq), INT_MIN, I32)
        sc = _index_scores(kin_ref[...], qi_ref, w_ref[...], h_idx, d_idx)
        S_ref[ncache, 0:tq, :] = _sort_key(sc)
        _select_top(S_ref, nck, n_sel, tq, idx_bits, cut_ref)
        _flash_init(m_ref, l_ref, acc_ref)

    def chunk_body(c, carry):
        cg = kb * cpb + c
        rows = pl.ds(pl.multiple_of(c * KC, KC), KC)
        kpos = cg * KC + lax.broadcasted_iota(I32, (KC, tq), 0)
        sdist = jnp.abs(qpos - kpos).astype(F32) * LOG2E
        vt = vc_ref[rows, :].T.astype(BF)
        _flash_update(pltpu.bitcast(S_ref[cg], F32), sdist, kc_ref[rows, :].astype(BF), vt, qt_ref,
                      m_ref, l_ref, acc_ref, lt_ref, p_ref, a_ref, n_heads, hd)
        return carry
    lax.fori_loop(0, cpb, chunk_body, 0)

    @pl.when(kb == nkb - 1)
    def _():
        tn = kn_ref.shape[0]
        kpos = past + lax.broadcasted_iota(I32, (tn, tq), 0)
        sdist = jnp.abs(past + lax.broadcasted_iota(I32, (tn, tq), 1) - kpos).astype(F32) * LOG2E
        vn = vn_ref[...]
        if tn % LANES:
            vn = jnp.concatenate([vn, jnp.zeros((LANES - tn % LANES, vn.shape[1]), F32)], axis=0)
        vt = vn.T[:, :tn].astype(BF)
        _flash_update(pltpu.bitcast(S_ref[ncache, 0:tn, :], F32), sdist, kn_ref[...].astype(BF), vt, qt_ref,
                      m_ref, l_ref, acc_ref, lt_ref, p_ref, a_ref, n_heads, hd)
        _flash_finish(o_ref, l_ref, acc_ref, n_heads, hd, tq)


def dsa_sample(proj, kcache, vcache, qi, ki, kicache, smt, l, *, n_batch, t_new, n_heads, hd, h_idx, d_idx, n_sel, tk):
    C = n_heads * hd
    past = kcache.shape[2]
    cpb = tk // KC
    nkb = past // tk
    idx_bits = max(1, int(np.ceil(np.log2(past + KC))))
    tq = t_new
    return pl.pallas_call(
        functools.partial(_dsa_sample_kernel, n_heads=n_heads, hd=hd, h_idx=h_idx, d_idx=d_idx, past=past,
                          n_sel=n_sel, cpb=cpb, nkb=nkb, idx_bits=idx_bits),
        grid=(n_batch, nkb),
        in_specs=[
            pl.BlockSpec((tq, C), lambda b, j: (b, 0)),
            pl.BlockSpec((tq, C), lambda b, j: (b, 1)),
            pl.BlockSpec((tq, C), lambda b, j: (b, 2)),
            pl.BlockSpec((None, None, tk, C), lambda b, j: (l, b, j, 0)),
            pl.BlockSpec((None, None, tk, C), lambda b, j: (l, b, j, 0)),
            pl.BlockSpec((tq, h_idx * d_idx), lambda b, j: (b, 0)),
            pl.BlockSpec((tq, d_idx), lambda b, j: (b, 0)),
            pl.BlockSpec((None, None, past, d_idx), lambda b, j: (l, b, 0, 0)),
            pl.BlockSpec((None, LANES, tq), lambda b, j: (b, 0, 0)),
        ],
        out_specs=pl.BlockSpec((tq, C), lambda b, j: (b, 0)),
        out_shape=jax.ShapeDtypeStruct((n_batch * tq, C), BF),
        scratch_shapes=[
            pltpu.VMEM((past // KC + 1, KC, tq), I32),
            pltpu.VMEM((1, tq), I32),
            pltpu.VMEM((C, tq), BF), pltpu.VMEM((SUBLANES, tq), F32),
            pltpu.VMEM((n_heads, 1, tq), F32), pltpu.VMEM((n_heads, 1, tq), F32),
            pltpu.VMEM((n_heads, hd, tq), F32),
            pltpu.VMEM((n_heads, KC, tq), F32), pltpu.VMEM((n_heads, KC, tq), BF),
            pltpu.VMEM((n_heads, 1, tq), F32),
        ],
        compiler_params=_cparams(("parallel", "arbitrary")),
        name="dsa_sample",
    )(proj, proj, proj, kcache, vcache, qi, ki, kicache, smt)


def _split3(x):
    hi = x.astype(BF)
    r = x - hi.astype(F32)
    mid = r.astype(BF)
    lo = (r - mid.astype(F32)).astype(BF)
    return hi, mid, lo


def _gdn_kernel(x_ref, gb_ref, sm_ref, cp_ref, s0_ref, cw_ref, ng_ref, ob_ref, sn_ref, S_ref, cbuf_ref,
                *, n_heads, hd, t_valid, conv_w, b_off, a_off, nchunks):
    c = pl.program_id(1)
    C = x_ref.shape[0]
    cd = n_heads * hd
    halo = conv_w - 1

    @pl.when(c == 0)
    def _():
        S_ref[...] = s0_ref[...]
        cbuf_ref[SUBLANES - halo:SUBLANES, :] = cp_ref[...]

    cbuf_ref[SUBLANES:SUBLANES + C, :] = x_ref[...]
    conv = None
    for i in range(conv_w):
        term = cbuf_ref[SUBLANES - halo + i:SUBLANES - halo + i + C, :] * cw_ref[i:i + 1, :]
        conv = term if conv is None else conv + term
    cbuf_ref[0:SUBLANES, :] = cbuf_ref[C:C + SUBLANES, :]
    act = conv * _sigmoid(conv)

    sm = sm_ref[...]
    row = c * C + lax.broadcasted_iota(I32, (C, LANES), 0)
    lane = lax.broadcasted_iota(I32, (C, LANES), 1)
    valid = row < t_valid
    la = jnp.where(valid & (lane >= a_off) & (lane < a_off + n_heads), sm, 0.0)
    beta_t = jnp.where(valid, sm, 0.0)
    ri = lax.broadcasted_iota(I32, (C, C), 0)
    ci = lax.broadcasted_iota(I32, (C, C), 1)
    causal = ri >= ci
    strict = ri > ci
    ltri = jnp.where(causal, 1.0, 0.0).astype(BF)
    g_t = sum(jnp.dot(ltri, piece, preferred_element_type=F32) for piece in _split3(la))
    g_rows = jnp.concatenate([g_t, jnp.zeros((LANES - C, LANES), F32)], axis=0).T

    H = range(n_heads)
    dot = functools.partial(jnp.dot, preferred_element_type=F32)
    dot_nt = functools.partial(lax.dot_general, dimension_numbers=NT_DIMS, preferred_element_type=F32)
    qh = [act[:, h * hd:(h + 1) * hd] for h in H]
    kh = [act[:, cd + h * hd:cd + (h + 1) * hd] for h in H]
    vh = [act[:, 2 * cd + h * hd:2 * cd + (h + 1) * hd] for h in H]
    qn = [qh[h] * lax.rsqrt(jnp.sum(qh[h] * qh[h], axis=-1, keepdims=True) + NORM_EPS) * (hd ** -0.5) for h in H]
    kn = [kh[h] * lax.rsqrt(jnp.sum(kh[h] * kh[h], axis=-1, keepdims=True) + NORM_EPS) for h in H]
    beta = [beta_t[:, b_off + h:b_off + h + 1] for h in H]
    g_c = [g_t[:, a_off + h:a_off + h + 1] for h in H]
    g_r = [g_rows[a_off + h:a_off + h + 1, :C] for h in H]
    g_last = [g_t[C - 1:C, a_off + h:a_off + h + 1] for h in H]
    decay = [jnp.where(causal, jnp.exp(jnp.where(causal, g_c[h] - g_r[h], 0.0)), 0.0) for h in H]
    eg = [jnp.exp(g_c[h]) for h in H]
    kb = [kn[h].astype(BF) for h in H]
    qb = [qn[h].astype(BF) for h in H]
    Sb = [S_ref[h].astype(BF) for h in H]
    kk = [dot_nt(kb[h], kb[h]) for h in H]
    qk = [dot_nt(qb[h], kb[h]) * decay[h] for h in H]
    o_state = [dot((qn[h] * eg[h]).astype(BF), Sb[h]) for h in H]
    N = [jnp.where(strict, -(beta[h] * kk[h] * decay[h]), 0.0) for h in H]
    P = list(N)
    for _ in range(int(np.log2(C)) - 1):
        Pb = [P[h].astype(BF) for h in H]
        P = [dot(Pb[h], Pb[h]) for h in H]
        NP = [dot(N[h].astype(BF), P[h].astype(BF)) for h in H]
        N = [N[h] + P[h] + NP[h] for h in H]
    rhs = [jnp.concatenate([vh[h] * beta[h], kn[h] * (beta[h] * eg[h])], axis=1) for h in H]
    corr = [dot(N[h].astype(BF), rhs[h].astype(BF)) for h in H]
    sol = [rhs[h] + corr[h] for h in H]
    su = [dot(sol[h][:, hd:].astype(BF), Sb[h]) for h in H]
    ub = [(sol[h][:, :hd] - su[h]).astype(BF) for h in H]
    o = [o_state[h] + dot(qk[h].astype(BF), ub[h]) for h in H]
    kd = [(kn[h] * jnp.exp(g_last[h] - g_c[h])).astype(BF) for h in H]
    upd = [lax.dot_general(kd[h], ub[h], TN_DIMS, preferred_element_type=F32) for h in H]
    for h in H:
        cols = slice(h * hd, (h + 1) * hd)
        S_ref[h] = S_ref[h] * jnp.exp(g_last[h]) + upd[h]
        gh = gb_ref[:, cols]
        on = o[h] * lax.rsqrt(jnp.mean(o[h] * o[h], axis=-1, keepdims=True) + NORM_EPS) * ng_ref[...]
        ob_ref[:, cols] = (on * (gh * _sigmoid(gh))).astype(BF)

    @pl.when(c == nchunks - 1)
    def _():
        sn_ref[...] = S_ref[...]


def gdn(proj, sm, conv_prev, s0, cw, ng, *, n_batch, rows_per_batch, t_valid, n_heads, hd, col0, b_off, a_off):
    cd = n_heads * hd
    nchunks = rows_per_batch // DN_C
    conv_w = cw.shape[0]
    qkv_blk = col0 // (3 * cd)
    g_blk = (col0 + 3 * cd) // cd
    return pl.pallas_call(
        functools.partial(_gdn_kernel, n_heads=n_heads, hd=hd, t_valid=t_valid, conv_w=conv_w,
                          b_off=b_off, a_off=a_off, nchunks=nchunks),
        grid=(n_batch, nchunks),
        in_specs=[
            pl.BlockSpec((DN_C, 3 * cd), lambda b, c: (b * nchunks + c, qkv_blk)),
            pl.BlockSpec((DN_C, cd), lambda b, c: (b * nchunks + c, g_blk)),
            pl.BlockSpec((DN_C, LANES), lambda b, c: (b * nchunks + c, 0)),
            pl.BlockSpec((None, conv_w - 1, 3 * cd), lambda b, c: (b, 0, 0)),
            pl.BlockSpec((None, n_heads, hd, hd), lambda b, c: (b, 0, 0, 0)),
            pl.BlockSpec((conv_w, 3 * cd), lambda b, c: (0, 0)),
            pl.BlockSpec((1, hd), lambda b, c: (0, 0)),
        ],
        out_specs=[pl.BlockSpec((DN_C, cd), lambda b, c: (b * nchunks + c, 0)),
                   pl.BlockSpec((None, n_heads, hd, hd), lambda b, c: (b, 0, 0, 0))],
        out_shape=[jax.ShapeDtypeStruct((n_batch * rows_per_batch, cd), BF),
                   jax.ShapeDtypeStruct((n_batch, n_heads, hd, hd), F32)],
        scratch_shapes=[pltpu.VMEM((n_heads, hd, hd), F32),
                        pltpu.VMEM((DN_C + SUBLANES, 3 * cd), F32)],
        compiler_params=_cparams(("parallel", "arbitrary")),
        name="gdn",
    )(proj, proj, sm, conv_prev, s0, cw, ng)


def _merge_kernel(h_ref, hb_ref, oa_ref, ob_ref, wga_ref, wgb_ref, woa_ref, wob_ref, wout_ref, g_ref, b_ref,
                  o_ref, acc_ref, *, alpha, nn, rs):
    j = pl.program_id(1)

    @pl.when(j == 0)
    def _():
        acc_ref[...] = jnp.zeros_like(acc_ref)

    hb = hb_ref[...]
    ga = _sigmoid(jnp.dot(hb, wga_ref[...], preferred_element_type=F32))
    gb = _sigmoid(jnp.dot(hb, wgb_ref[...], preferred_element_type=F32))
    y = (ga * jnp.dot(oa_ref[...], woa_ref[...], preferred_element_type=F32)
         + gb * jnp.dot(ob_ref[...], wob_ref[...], preferred_element_type=F32))
    acc_ref[...] += jnp.dot(y.astype(BF), wout_ref[...], preferred_element_type=F32)

    @pl.when(j == nn - 1)
    def _():
        def slab(r, carry):
            rows = pl.ds(pl.multiple_of(r * rs, rs), rs)
            o_ref[rows, :] = _layer_norm(alpha * h_ref[rows, :] + acc_ref[rows, :], g_ref[...], b_ref[...])
            return carry
        lax.fori_loop(0, h_ref.shape[0] // rs, slab, 0)


def merge_ln(h, hb, oa, ob, wgate, woa, wob, wout, g, b, l, alpha, tm, tn):
    M, D = h.shape
    ca, cb = oa.shape[1], ob.shape[1]
    nn = D // tn
    rs = 64 if tm % 64 == 0 else 16
    return pl.pallas_call(
        functools.partial(_merge_kernel, alpha=alpha, nn=nn, rs=rs),
        grid=(M // tm, nn),
        in_specs=[
            pl.BlockSpec((tm, D), lambda i, j: (i, 0)),
            pl.BlockSpec((tm, D), lambda i, j: (i, 0)),
            pl.BlockSpec((tm, ca), lambda i, j: (i, 0)),
            pl.BlockSpec((tm, cb), lambda i, j: (i, 0)),
            pl.BlockSpec((None, D, tn), lambda i, j: (l, 0, j)),
            pl.BlockSpec((None, D, tn), lambda i, j: (l, 0, nn + j)),
            pl.BlockSpec((None, ca, tn), lambda i, j: (l, 0, j)),
            pl.BlockSpec((None, cb, tn), lambda i, j: (l, 0, j)),
            pl.BlockSpec((None, tn, D), lambda i, j: (l, j, 0)),
            pl.BlockSpec((1, D), lambda i, j: (0, 0)),
            pl.BlockSpec((1, D), lambda i, j: (0, 0)),
        ],
        out_specs=pl.BlockSpec((tm, D), lambda i, j: (i, 0)),
        out_shape=jax.ShapeDtypeStruct((M, D), F32),
        scratch_shapes=[pltpu.VMEM((tm, D), F32)],
        compiler_params=_cparams(("parallel", "arbitrary")),
        name="merge_ln",
    )(h, hb, oa, ob, wgate, wgate, woa, wob, wout, g, b)


def _round_up(n, m):
    return -(-n // m) * m


def kernel(x_prompt, x_sample, cache_k, cache_v, cache_kidx, state_conv, state_dn, meta, w_ff_gu, w_ff_down, ln_g, ln_b, w_in, conv_w, idx_ln_g, idx_ln_b, a_log, dt_bias, dn_norm_g, w_oa, w_ob, w_out):
    depth = w_in.shape[0]
    B, S, D = x_prompt.shape
    Bs, Ts, _ = x_sample.shape
    n_meta = meta.shape[0]
    past, HA, HDA = cache_k.shape[2], cache_k.shape[3], cache_k.shape[4]
    CA = HA * HDA
    d_idx = cache_kidx.shape[-1]
    HB, HDB = state_dn.shape[2], state_dn.shape[3]
    CD = HB * HDB
    cw_len = conv_w.shape[1]
    F = w_ff_down.shape[2]
    n_in = w_in.shape[2]
    h_idx = (n_in - 3 * CA - d_idx - 4 * CD - 2 * HB - 2 * D) // (d_idx + 1)
    assert B == 1 and Ts == DN_C and CHUNK == 64 and n_meta <= CHUNK
    assert d_idx + h_idx + 2 * HB <= LANES and h_idx <= SUBLANES and HA * HDA == CA
    assert past % KC == 0 and CA == CD
    alpha = float((2 * depth) ** 0.25)
    T = n_meta + S
    Mp = _round_up(T, ROW_PAD)
    n_sel_p = min(MAX_SEL, S // 4)
    n_sel_s = min(MAX_SEL, (past + Ts) // 4)
    w_off, b_off, a_off = d_idx, d_idx + h_idx, d_idx + h_idx + HB

    Fp = _round_up(F, FF_ALIGN)
    padf = ((0, 0), (0, 0), (0, 0), (0, Fp - F))
    wg = jnp.pad(w_ff_gu[..., :F].astype(BF), padf)
    wu = jnp.pad(w_ff_gu[..., F:].astype(BF), padf)
    wd = jnp.pad(w_ff_down.astype(BF), ((0, 0), (0, 0), (0, Fp - F), (0, 0)))
    offs = np.cumsum([0, CA, CA, CA, h_idx * d_idx, d_idx, h_idx, 3 * CD, CD, HB, HB, D, D])
    span = lambda i, j: w_in[:, :, offs[i]:offs[j]].astype(BF)
    w_main = jnp.concatenate([span(0, 3), span(6, 8)], axis=2)
    pad_small = LANES - (d_idx + h_idx + 2 * HB)
    w_idx = jnp.concatenate([span(3, 6), span(8, 10), jnp.zeros((depth, D, pad_small), BF)], axis=2)
    w_gate = span(10, 12)
    woa, wob, wout = w_oa.astype(BF), w_ob.astype(BF), w_out.astype(BF)

    def lane_vec(v, off):
        return jnp.zeros((depth, 1, LANES), F32).at[:, 0, off:off + v.shape[1]].set(v)
    lng_v, lnb_v = lane_vec(idx_ln_g, 0), lane_vec(idx_ln_b, 0)
    alog_v, dtb_v = lane_vec(a_log, a_off), lane_vec(dt_bias, a_off)

    xp = jnp.concatenate([meta.astype(F32), x_prompt[0], jnp.zeros((Mp - T, D), F32)], axis=0)
    xs = x_sample.reshape(Bs * Ts, D)
    Ms = Bs * Ts
    tm_p, tm_s = _pick_tile(Mp, TM_FFN), _pick_tile(Ms, 512)
    tmm_p, tmm_s = _pick_tile(Mp, TM_MM), _pick_tile(Ms, 1024)
    tf = FF_ALIGN
    tn_mm = _pick_tile(w_main.shape[2], TN_MM)
    tn_mg = _pick_tile(D, TN_MERGE)
    tq = _pick_tile(Mp, TQ)
    tk_p = _pick_tile(Mp, TK_PROMPT)
    tk_c = _pick_tile(past, TK_CACHE)
    assert tq % LANES == 0 and tk_p % KC == 0 and tk_c % KC == 0 and Mp % KC == 0

    kcache = cache_k.reshape(depth, Bs, past, CA)
    vcache = cache_v.reshape(depth, Bs, past, CA)
    kicache = cache_kidx.astype(BF)
    conv0 = jnp.zeros((1, cw_len - 1, 3 * CD), F32)
    s0 = jnp.zeros((1, HB, HDB, HDB), F32)

    outs = [[] for _ in range(10)]
    for l in range(depth):
        g3 = [ln_g[l, i][None] for i in range(3)]
        b3 = [ln_b[l, i][None] for i in range(3)]
        ng = dn_norm_g[l][None]
        streams = []
        for (x, is_prompt) in ((xp, True), (xs, False)):
            tm, tmm = (tm_p, tmm_p) if is_prompt else (tm_s, tmm_s)
            h, hb = ffn_ln(x, wg, wu, wd, g3[0], b3[0], l, 0, alpha, tm, tf)
            proj = matmul(hb, w_main, l, tmm, tn_mm)
            qi, ki, sm = idx_proj(hb, w_idx, lng_v[l], lnb_v[l], alog_v[l], dtb_v[l], l, tmm, d_idx, h_idx, HB)
            if is_prompt:
                kb16 = proj[:, CA:2 * CA].astype(BF)
                vt4 = proj[:, 2 * CA:3 * CA].astype(BF).reshape(Mp // KC, KC, HA, HDA).transpose(0, 2, 3, 1)
                vt3 = jnp.concatenate([vt4, jnp.ones((Mp // KC, HA, ONES_ROWS, KC), BF)], axis=2)
                vt3 = vt3.reshape(Mp // KC, HA * (HDA + ONES_ROWS), KC)
                oa = dsa_prompt(proj, kb16, vt3, qi, ki, sm.T, t_valid=T, n_meta=n_meta, n_heads=HA, hd=HDA,
                                h_idx=h_idx, d_idx=d_idx, n_sel=n_sel_p, tq=tq, tk=tk_p)
                ob, s_new = gdn(proj, sm, conv0, s0, conv_w[l], ng, n_batch=1, rows_per_batch=Mp, t_valid=T,
                                n_heads=HB, hd=HDB, col0=3 * CA, b_off=b_off, a_off=a_off)
                nrow = T
            else:
                smt = sm.reshape(Bs, Ts, LANES).swapaxes(1, 2)
                oa = dsa_sample(proj, kcache, vcache, qi, ki, kicache, smt, l, n_batch=Bs, t_new=Ts,
                                n_heads=HA, hd=HDA, h_idx=h_idx, d_idx=d_idx, n_sel=n_sel_s, tk=tk_c)
                ob, s_new = gdn(proj, sm, state_conv[l], state_dn[l], conv_w[l], ng, n_batch=Bs, rows_per_batch=Ts,
                                t_valid=Ts, n_heads=HB, hd=HDB, col0=3 * CA, b_off=b_off, a_off=a_off)
                nrow = Ms
            x = merge_ln(h, hb, oa, ob, w_gate, woa, wob, wout, g3[1], b3[1], l, alpha, tm, tn_mg)
            x, _ = ffn_ln(x, wg, wu, wd, g3[2], b3[2], l, 1, alpha, tm, tf)
            nb = 1 if is_prompt else Bs
            per = nrow // nb
            k_new = proj[:nrow, CA:2 * CA].reshape(nb, per, HA, HDA)
            v_new = proj[:nrow, 2 * CA:3 * CA].reshape(nb, per, HA, HDA)
            ki_new = sm[:nrow, :d_idx].reshape(nb, per, d_idx)
            qkv_b = proj[:nrow, 3 * CA:3 * CA + 3 * CD].reshape(nb, per, 3 * CD)
            if is_prompt:
                conv_new = qkv_b[:, per - (cw_len - 1):]
            else:
                conv_new = jnp.concatenate([state_conv[l], qkv_b], axis=1)[:, -(cw_len - 1):]
            streams.append((x, k_new, v_new, ki_new, conv_new, s_new))
        (xp, *st_p), (xs, *st_s) = streams
        for i, a in enumerate(st_p):
            outs[i].append(a)
        for i, a in enumerate(st_s):
            outs[5 + i].append(a)
    y_prompt = xp[n_meta:T][None]
    y_sample = xs.reshape(Bs, Ts, D)
    return (y_prompt, y_sample) + tuple(jnp.stack(o) for o in outs)
```

```python
import functools

import numpy as np
import jax
import jax.numpy as jnp
from jax import lax
from jax.experimental import pallas as pl
from jax.experimental.pallas import tpu as pltpu

CHUNK = 64
MAX_SEL = 256
DN_C = 64
LN_EPS = 1e-5
NORM_EPS = 1e-6

LANES = 128
SUBLANES = 8
KC = 256
FF_ALIGN = 512
VMEM_LIMIT = 56 << 20

ROW_PAD = 1280
TM_FFN = 640
TM_MM = 1280
TN_MM = 1024
TN_MERGE = 512
TQ = 256
TK_PROMPT = 1280
TK_CACHE = 1024

BF = jnp.bfloat16
F32 = jnp.float32
I32 = jnp.int32
I16 = jnp.int16
I16_OFF = 2 ** 15
PACK16 = 16
INT_MIN = -(2 ** 31)
NEG = -1e30
LOG2E = 1.4426950408889634
ONES_ROWS = 16
NT_DIMS = (((1,), (1,)), ((), ()))
TN_DIMS = (((0,), (0,)), ((), ()))


def _cparams(sem):
    return pltpu.CompilerParams(dimension_semantics=sem, vmem_limit_bytes=VMEM_LIMIT)


def _sigmoid(x):
    return 1.0 / (1.0 + jnp.exp(-x))


def _layer_norm(y, g, b):
    mu = jnp.mean(y, axis=-1, keepdims=True)
    d = y - mu
    var = jnp.mean(d * d, axis=-1, keepdims=True)
    return d * lax.rsqrt(var + LN_EPS) * g + b


def _pick_tile(n, pref):
    if n <= pref:
        return n
    for t in range(pref, 15, -16):
        if n % t == 0:
            return t
    return n


def _ffn_kernel(x_ref, wg_ref, wu_ref, wd_ref, g_ref, b_ref, o_ref, ob_ref, xb_ref, acc_ref, *, alpha, nf, rs):
    j = pl.program_id(1)

    @pl.when(j == 0)
    def _():
        xb_ref[...] = x_ref[...].astype(BF)
        acc_ref[...] = jnp.zeros_like(acc_ref)

    xb = xb_ref[...]
    hg = jnp.dot(xb, wg_ref[...], preferred_element_type=F32)
    hu = jnp.dot(xb, wu_ref[...], preferred_element_type=F32)
    a = (hg * _sigmoid(hg) * hu).astype(BF)
    acc_ref[...] += jnp.dot(a, wd_ref[...], preferred_element_type=F32)

    @pl.when(j == nf - 1)
    def _():
        def slab(r, carry):
            rows = pl.ds(pl.multiple_of(r * rs, rs), rs)
            y = alpha * x_ref[rows, :] + 0.5 * acc_ref[rows, :]
            z = _layer_norm(y, g_ref[...], b_ref[...])
            o_ref[rows, :] = z
            ob_ref[rows, :] = z.astype(BF)
            return carry
        lax.fori_loop(0, x_ref.shape[0] // rs, slab, 0)


def ffn_ln(x, wg, wu, wd, g, b, l, s, alpha, tm, tf):
    M, D = x.shape
    nf = wd.shape[2] // tf
    rs = 64 if tm % 64 == 0 else 16
    return pl.pallas_call(
        functools.partial(_ffn_kernel, alpha=alpha, nf=nf, rs=rs),
        grid=(M // tm, nf),
        in_specs=[
            pl.BlockSpec((tm, D), lambda i, j: (i, 0)),
            pl.BlockSpec((None, None, D, tf), lambda i, j: (l, s, 0, j)),
            pl.BlockSpec((None, None, D, tf), lambda i, j: (l, s, 0, j)),
            pl.BlockSpec((None, None, tf, D), lambda i, j: (l, s, j, 0)),
            pl.BlockSpec((1, D), lambda i, j: (0, 0)),
            pl.BlockSpec((1, D), lambda i, j: (0, 0)),
        ],
        out_specs=[pl.BlockSpec((tm, D), lambda i, j: (i, 0)), pl.BlockSpec((tm, D), lambda i, j: (i, 0))],
        out_shape=[jax.ShapeDtypeStruct((M, D), F32), jax.ShapeDtypeStruct((M, D), BF)],
        scratch_shapes=[pltpu.VMEM((tm, D), BF), pltpu.VMEM((tm, D), F32)],
        compiler_params=_cparams(("parallel", "arbitrary")),
        name="ffn_ln",
    )(x, wg, wu, wd, g, b)


def _mm_kernel(x_ref, w_ref, o_ref):
    o_ref[...] = jnp.dot(x_ref[...], w_ref[...], preferred_element_type=F32)


def matmul(xb, w, l, tm, tn):
    M, D = xb.shape
    N = w.shape[2]
    return pl.pallas_call(
        _mm_kernel,
        grid=(M // tm, N // tn),
        in_specs=[pl.BlockSpec((tm, D), lambda i, j: (i, 0)),
                  pl.BlockSpec((None, D, tn), lambda i, j: (l, 0, j))],
        out_specs=pl.BlockSpec((tm, tn), lambda i, j: (i, j)),
        out_shape=jax.ShapeDtypeStruct((M, N), F32),
        compiler_params=_cparams(("parallel", "arbitrary")),
        name="in_proj",
    )(xb, w)


def _idx_kernel(x_ref, w_ref, lng_ref, lnb_ref, alog_ref, dtb_ref, qi_ref, ki_ref, sm_ref, *, n_qi, d_idx, h_idx, h_b):
    r = jnp.dot(x_ref[...], w_ref[...], preferred_element_type=F32)
    qi_ref[...] = r[:, :n_qi].astype(BF)
    t = r[:, n_qi:]
    lane = lax.broadcasted_iota(I32, t.shape, 1)
    isk = lane < d_idx
    mu = jnp.sum(jnp.where(isk, t, 0.0), axis=-1, keepdims=True) / d_idx
    d = jnp.where(isk, t - mu, 0.0)
    var = jnp.sum(d * d, axis=-1, keepdims=True) / d_idx
    kn = d * lax.rsqrt(var + LN_EPS) * lng_ref[...] + lnb_ref[...]
    wv = t * (h_idx ** -0.5)
    beta = _sigmoid(t)
    z = t + dtb_ref[...]
    softplus = jnp.maximum(z, 0.0) + jnp.log(1.0 + jnp.exp(-jnp.abs(z)))
    la = -jnp.exp(alog_ref[...]) * softplus
    w_off, b_off, a_off = d_idx, d_idx + h_idx, d_idx + h_idx + h_b
    out = jnp.where(isk, kn,
                    jnp.where(lane < b_off, wv,
                              jnp.where(lane < a_off, beta,
                                        jnp.where(lane < a_off + h_b, la, 0.0))))
    sm_ref[...] = out
    ki_ref[...] = kn[:, :d_idx].astype(BF)


def idx_proj(xb, w, lng, lnb, alog, dtb, l, tm, d_idx, h_idx, h_b):
    M, D = xb.shape
    N = w.shape[2]
    n_qi = N - LANES
    vec = pl.BlockSpec((1, LANES), lambda i: (0, 0))
    return pl.pallas_call(
        functools.partial(_idx_kernel, n_qi=n_qi, d_idx=d_idx, h_idx=h_idx, h_b=h_b),
        grid=(M // tm,),
        in_specs=[pl.BlockSpec((tm, D), lambda i: (i, 0)),
                  pl.BlockSpec((None, D, N), lambda i: (l, 0, 0)),
                  vec, vec, vec, vec],
        out_specs=[pl.BlockSpec((tm, n_qi), lambda i: (i, 0)),
                   pl.BlockSpec((tm, d_idx), lambda i: (i, 0)),
                   pl.BlockSpec((tm, LANES), lambda i: (i, 0))],
        out_shape=[jax.ShapeDtypeStruct((M, n_qi), BF),
                   jax.ShapeDtypeStruct((M, d_idx), BF),
                   jax.ShapeDtypeStruct((M, LANES), F32)],
        compiler_params=_cparams(("parallel",)),
        name="idx_proj",
    )(xb, w, lng, lnb, alog, dtb)


def _sort_key(s):
    b = pltpu.bitcast(s + 0.0, I32)
    return b ^ ((b >> 31) & 0x7FFFFFFF)


def _index_scores(ki_chunk, qi_ref, w_rows, h_idx, d_idx):
    s = [lax.dot_general(ki_chunk, qi_ref[:, h * d_idx:(h + 1) * d_idx], NT_DIMS, preferred_element_type=F32)
         for h in range(h_idx)]
    acc = w_rows[0:1, :] * jnp.maximum(s[0], 0.0)
    for h in range(1, h_idx):
        acc = acc + w_rows[h:h + 1, :] * jnp.maximum(s[h], 0.0)
    return acc


def _store_key(hi_ref, lo_ref, where, key):
    hi_ref[where] = (key >> 16).astype(I16)
    lo_ref[where] = ((key & 0xFFFF) - I16_OFF).astype(I16)


def _count16(nck, tq, pred):
    def body(c, cnt):
        ones = jnp.where(pred(c), jnp.int16(1), jnp.int16(0))
        parts = [ones[r * PACK16:(r + 1) * PACK16] for r in range(ones.shape[0] // PACK16)]
        while len(parts) > 1:
            parts = [a + b for a, b in zip(parts[::2], parts[1::2])]
        return cnt + parts[0]
    cnt16 = lax.fori_loop(0, nck, body, jnp.zeros((PACK16, tq), I16))
    return cnt16.astype(I32).sum(axis=0, keepdims=True)


def _search16(nck, tq, n_sel, base, cnt0, count_ge):
    def bit_body(i, carry):
        thr, cnt_thr = carry
        cand = thr + jnp.left_shift(jnp.int32(1), 15 - i)
        cnt = base + count_ge(cand.astype(I16))
        ok = cnt >= n_sel
        return jnp.where(ok, cand, thr), jnp.where(ok, cnt, cnt_thr)
    return lax.fori_loop(0, 16, bit_body, (jnp.full((1, tq), -I16_OFF, I32), cnt0))


def _select_top(hi_ref, lo_ref, nck, n_sel, tq, idx_bits, cut_ref):
    kc = hi_ref.shape[1]
    zero = jnp.zeros((1, tq), I32)
    thr_hi, cnt_ge_hi = _search16(nck, tq, n_sel, zero, zero,
                                  lambda t: _count16(nck, tq, lambda c: hi_ref[c] >= t))
    t_hi = thr_hi.astype(I16)
    cnt_gt_hi = _count16(nck, tq, lambda c: hi_ref[c] > t_hi)

    def lo_body(c, carry):
        lo_ref[c] = jnp.where(hi_ref[c] == t_hi, lo_ref[c], jnp.int16(-I16_OFF))
        return carry
    lax.fori_loop(0, nck, lo_body, 0)
    thr_lo, cnt_thr = _search16(nck, tq, n_sel, cnt_gt_hi, cnt_ge_hi,
                                lambda t: _count16(nck, tq, lambda c: lo_ref[c] >= t))
    t_lo = thr_lo.astype(I16)
    has_tie = (thr_hi > -I16_OFF) & (cnt_thr > n_sel)
    cut_ref[...] = jnp.full((1, tq), I16_OFF - 1, I32)

    def kidx16(c):
        return (c * kc + lax.broadcasted_iota(I32, (kc, tq), 0)).astype(I16)

    @pl.when(jnp.max(has_tie.astype(I32)) > 0)
    def _():
        need = n_sel - cnt_gt_hi - _count16(nck, tq, lambda c: (hi_ref[c] == t_hi) & (lo_ref[c] > t_lo))

        def idx_body(i, cut):
            cand = cut + jnp.left_shift(jnp.int32(1), idx_bits - 1 - i)
            c16 = cand.astype(I16)
            cnt = _count16(nck, tq, lambda c: (hi_ref[c] == t_hi) & (lo_ref[c] == t_lo) & (kidx16(c) < c16))
            return jnp.where(cnt < need, cand, cut)

        cut = lax.fori_loop(0, idx_bits, idx_body, zero)
        cut_ref[...] = jnp.where(has_tie, cut, I16_OFF - 1)

    cut = cut_ref[...].astype(I16)

    def bias_body(c, carry):
        hi, lo = hi_ref[c], lo_ref[c]
        sel = (hi > t_hi) | ((hi == t_hi) & ((lo > t_lo) | ((lo == t_lo) & (kidx16(c) <= cut))))
        sel = sel & (hi > jnp.int16(-I16_OFF))
        hi_ref[c] = pltpu.bitcast(jnp.where(sel, jnp.zeros((), BF), jnp.full((), NEG, BF)), I16)
        return carry
    lax.fori_loop(0, nck, bias_body, 0)


def _flash_update(bias16, sdist, k_heads, vt_heads, vrows, qt_ref, m_ref, l_ref, acc_ref, lt_ref, p_ref, a_ref,
                  n_heads, hd):
    kc = sdist.shape[0]
    bias = pltpu.bitcast(bias16, BF).astype(F32)
    for h in range(n_heads):
        cols = slice(h * hd, (h + 1) * hd)
        lt_ref[h, 0:kc, :] = jnp.dot(k_heads[h](), qt_ref[cols, :], preferred_element_type=F32)
    for h in range(n_heads):
        slope = float(2.0 ** (-(8.0 / n_heads) * (h + 1)))
        lt = lt_ref[h, 0:kc, :] + (bias - slope * sdist)
        m_old = m_ref[h]
        m_new = jnp.maximum(m_old, jnp.max(lt, axis=0, keepdims=True))
        p = jnp.exp2(lt - m_new)
        a = jnp.exp2(m_old - m_new)
        if vrows == hd:
            l_ref[h] = a * l_ref[h] + jnp.sum(p, axis=0, keepdims=True)
        p_ref[h, 0:kc, :] = p.astype(BF)
        a_ref[h] = a
        m_ref[h] = m_new
    for h in range(n_heads):
        pv = jnp.dot(vt_heads[h](), p_ref[h, 0:kc, :], preferred_element_type=F32)
        a = a_ref[h]
        acc_ref[h] = a * acc_ref[h] + pv[:hd]
        if vrows > hd:
            l_ref[h] = a * l_ref[h] + pv[hd:hd + 1]


def _flash_init(m_ref, l_ref, acc_ref):
    m_ref[...] = jnp.full(m_ref.shape, NEG, F32)
    l_ref[...] = jnp.zeros(l_ref.shape, F32)
    acc_ref[...] = jnp.zeros(acc_ref.shape, F32)


def _flash_finish(o_ref, l_ref, acc_ref, n_heads, hd, tq):
    for h in range(n_heads):
        ot = acc_ref[h] / l_ref[h]
        if tq % LANES:
            ot = jnp.concatenate([ot, jnp.zeros((hd, LANES - tq % LANES), F32)], axis=1)
        o_ref[:, h * hd:(h + 1) * hd] = ot.T[:tq].astype(BF)


def _dsa_prompt_kernel(qmap, kmap, nckq, q_ref, k_ref, vt_ref, qi_ref, ki_ref, smt_ref, o_ref,
                       hi_ref, lo_ref, cut_ref, qt_ref, w_ref, m_ref, l_ref, acc_ref, lt_ref, p_ref, a_ref,
                       *, n_heads, hd, h_idx, d_idx, n_meta, t_valid, n_sel, cpb, idx_bits):
    p = pl.program_id(0)
    qb, kb = qmap[p], kmap[p]
    nck = nckq[qb]
    tq = q_ref.shape[0]
    qpos = qb * tq + lax.broadcasted_iota(I32, (KC, tq), 1)

    @pl.when(kb == 0)
    def _():
        qt_ref[...] = (q_ref[...] * (hd ** -0.5 * LOG2E)).T.astype(BF)
        w_ref[...] = smt_ref[d_idx:d_idx + SUBLANES, :] * (d_idx ** -0.5)
        qchk = jnp.maximum((qpos - n_meta) >> 6, -1)

        def score_body(c, carry):
            k0 = pl.multiple_of(c * KC, KC)
            sc = _index_scores(ki_ref[pl.ds(k0, KC), :], qi_ref, w_ref[...], h_idx, d_idx)
            kpos = k0 + lax.broadcasted_iota(I32, (KC, tq), 0)
            allowed = (jnp.maximum((kpos - n_meta) >> 6, -1) <= qchk) & (kpos < t_valid)
            _store_key(hi_ref, lo_ref, c, jnp.where(allowed, _sort_key(sc), INT_MIN))
            return carry
        lax.fori_loop(0, nck, score_body, 0)
        _select_top(hi_ref, lo_ref, nck, n_sel, tq, idx_bits, cut_ref)
        _flash_init(m_ref, l_ref, acc_ref)

    vr = vt_ref.shape[1] // n_heads

    def chunk_body(c, carry):
        cg = kb * cpb + c
        rows = pl.ds(pl.multiple_of(c * KC, KC), KC)
        kpos = cg * KC + lax.broadcasted_iota(I32, (KC, tq), 0)
        sdist = jnp.abs(qpos - kpos).astype(F32) * LOG2E
        k_heads = [functools.partial(lambda h: k_ref[rows, h * hd:(h + 1) * hd], h) for h in range(n_heads)]
        vt_heads = [functools.partial(lambda h: vt_ref[c, h * vr:(h + 1) * vr, :], h) for h in range(n_heads)]
        _flash_update(hi_ref[cg], sdist, k_heads, vt_heads, vr, qt_ref,
                      m_ref, l_ref, acc_ref, lt_ref, p_ref, a_ref, n_heads, hd)
        return carry
    lax.fori_loop(0, jnp.minimum(cpb, nck - kb * cpb), chunk_body, 0)

    @pl.when((kb + 1) * cpb >= nck)
    def _():
        _flash_finish(o_ref, l_ref, acc_ref, n_heads, hd, tq)


def dsa_prompt(proj, kb16, vt3, qi, ki, smt, *, t_valid, n_meta, n_heads, hd, h_idx, d_idx, n_sel, tq, tk):
    Mp = proj.shape[0]
    C = n_heads * hd
    vr = vt3.shape[1]
    cpb = tk // KC
    nq = Mp // tq
    qlast = np.minimum(np.arange(nq) * tq + tq - 1, t_valid - 1)
    kend = np.minimum(t_valid, n_meta + CHUNK * (np.maximum((qlast - n_meta) // CHUNK, -1) + 1))
    nckq = -(-kend // KC)
    pairs = [(i, j) for i in range(nq) for j in range(-(-int(nckq[i]) // cpb))]
    qmap = jnp.asarray([a for a, _ in pairs], I32)
    kmap = jnp.asarray([b for _, b in pairs], I32)
    idx_bits = max(1, int(np.ceil(np.log2(Mp))))
    grid_spec = pltpu.PrefetchScalarGridSpec(
        num_scalar_prefetch=3,
        grid=(len(pairs),),
        in_specs=[
            pl.BlockSpec((tq, C), lambda p, qm, km, nk: (qm[p], 0)),
            pl.BlockSpec((tk, C), lambda p, qm, km, nk: (km[p], 0)),
            pl.BlockSpec((cpb, vr, KC), lambda p, qm, km, nk: (km[p], 0, 0)),
            pl.BlockSpec((tq, h_idx * d_idx), lambda p, qm, km, nk: (qm[p], 0)),
            pl.BlockSpec((Mp, d_idx), lambda p, qm, km, nk: (0, 0)),
            pl.BlockSpec((LANES, tq), lambda p, qm, km, nk: (0, qm[p])),
        ],
        out_specs=pl.BlockSpec((tq, C), lambda p, qm, km, nk: (qm[p], 0)),
        scratch_shapes=[
            pltpu.VMEM((Mp // KC, KC, tq), I16), pltpu.VMEM((Mp // KC, KC, tq), I16),
            pltpu.VMEM((1, tq), I32),
            pltpu.VMEM((C, tq), BF), pltpu.VMEM((SUBLANES, tq), F32),
            pltpu.VMEM((n_heads, 1, tq), F32), pltpu.VMEM((n_heads, 1, tq), F32),
            pltpu.VMEM((n_heads, hd, tq), F32),
            pltpu.VMEM((n_heads, KC, tq), F32), pltpu.VMEM((n_heads, KC, tq), BF),
            pltpu.VMEM((n_heads, 1, tq), F32),
        ],
    )
    return pl.pallas_call(
        functools.partial(_dsa_prompt_kernel, n_heads=n_heads, hd=hd, h_idx=h_idx, d_idx=d_idx, n_meta=n_meta,
                          t_valid=t_valid, n_sel=n_sel, cpb=cpb, idx_bits=idx_bits),
        grid_spec=grid_spec,
        out_shape=jax.ShapeDtypeStruct((Mp, C), BF),
        compiler_params=_cparams(("arbitrary",)),
        name="dsa_prompt",
    )(qmap, kmap, jnp.asarray(nckq, I32), proj, kb16, vt3, qi, ki, smt)


def _dsa_sample_kernel(q_ref, kn_ref, vn_ref, kc_ref, vc_ref, qi_ref, kin_ref, kic_ref, smt_ref, o_ref,
                       hi_ref, lo_ref, cut_ref, qt_ref, w_ref, m_ref, l_ref, acc_ref, lt_ref, p_ref, a_ref,
                       *, n_heads, hd, h_idx, d_idx, past, n_sel, cpb, nkb, idx_bits):
    kb = pl.program_id(1)
    tq = q_ref.shape[0]
    ncache = past // KC
    nck = ncache + 1
    qpos = past + lax.broadcasted_iota(I32, (KC, tq), 1)

    @pl.when(kb == 0)
    def _():
        qs = q_ref[...] * (hd ** -0.5 * LOG2E)
        if tq % LANES:
            qs = jnp.concatenate([qs, jnp.zeros((LANES - tq % LANES, qs.shape[1]), F32)], axis=0)
        qt_ref[...] = qs.T[:, :tq].astype(BF)
        w_ref[...] = smt_ref[d_idx:d_idx + SUBLANES, :] * (d_idx ** -0.5)

        def score_body(c, carry):
            k0 = pl.multiple_of(c * KC, KC)
            sc = _index_scores(kic_ref[pl.ds(k0, KC), :], qi_ref, w_ref[...], h_idx, d_idx)
            _store_key(hi_ref, lo_ref, c, _sort_key(sc))
            return carry
        lax.fori_loop(0, ncache, score_body, 0)
        hi_ref[ncache] = jnp.full((KC, tq), -I16_OFF, I16)
        lo_ref[ncache] = jnp.full((KC, tq), -I16_OFF, I16)
        sc = _index_scores(kin_ref[...], qi_ref, w_ref[...], h_idx, d_idx)
        _store_key(hi_ref, lo_ref, (ncache, slice(0, tq), slice(None)), _sort_key(sc))
        _select_top(hi_ref, lo_ref, nck, n_sel, tq, idx_bits, cut_ref)
        _flash_init(m_ref, l_ref, acc_ref)

    vr = vc_ref.shape[1] // n_heads

    def chunk_body(c, carry):
        cg = kb * cpb + c
        rows = pl.ds(pl.multiple_of(c * KC, KC), KC)
        kpos = cg * KC + lax.broadcasted_iota(I32, (KC, tq), 0)
        sdist = jnp.abs(qpos - kpos).astype(F32) * LOG2E
        k_heads = [functools.partial(lambda h: kc_ref[rows, h * hd:(h + 1) * hd], h) for h in range(n_heads)]
        vt_heads = [functools.partial(lambda h: vc_ref[c, h * vr:(h + 1) * vr, :], h) for h in range(n_heads)]
        _flash_update(hi_ref[cg], sdist, k_heads, vt_heads, vr, qt_ref,
                      m_ref, l_ref, acc_ref, lt_ref, p_ref, a_ref, n_heads, hd)
        return carry
    lax.fori_loop(0, cpb, chunk_body, 0)

    @pl.when(kb == nkb - 1)
    def _():
        tn = kn_ref.shape[0]
        kpos = past + lax.broadcasted_iota(I32, (tn, tq), 0)
        sdist = jnp.abs(past + lax.broadcasted_iota(I32, (tn, tq), 1) - kpos).astype(F32) * LOG2E

        def vt_new(h):
            vn = vn_ref[:, h * hd:(h + 1) * hd]
            if tn % LANES:
                vn = jnp.concatenate([vn, jnp.zeros((LANES - tn % LANES, hd), F32)], axis=0)
            return vn.T[:, :tn].astype(BF)
        k_heads = [functools.partial(lambda h: kn_ref[:, h * hd:(h + 1) * hd].astype(BF), h) for h in range(n_heads)]
        vt_heads = [functools.partial(vt_new, h) for h in range(n_heads)]
        _flash_update(hi_ref[ncache, 0:tn, :], sdist, k_heads, vt_heads, hd, qt_ref,
                      m_ref, l_ref, acc_ref, lt_ref, p_ref, a_ref, n_heads, hd)
        _flash_finish(o_ref, l_ref, acc_ref, n_heads, hd, tq)


def dsa_sample(proj, kcache, vcache, qi, ki, kicache, smt, l, *, n_batch, t_new, n_heads, hd, h_idx, d_idx, n_sel, tk):
    C = n_heads * hd
    past = kcache.shape[2]
    cpb = tk // KC
    nkb = past // tk
    idx_bits = max(1, int(np.ceil(np.log2(past + KC))))
    tq = t_new
    return pl.pallas_call(
        functools.partial(_dsa_sample_kernel, n_heads=n_heads, hd=hd, h_idx=h_idx, d_idx=d_idx, past=past,
                          n_sel=n_sel, cpb=cpb, nkb=nkb, idx_bits=idx_bits),
        grid=(n_batch, nkb),
        in_specs=[
            pl.BlockSpec((tq, C), lambda b, j: (b, 0)),
            pl.BlockSpec((tq, C), lambda b, j: (b, 1)),
            pl.BlockSpec((tq, C), lambda b, j: (b, 2)),
            pl.BlockSpec((None, None, tk, C), lambda b, j: (l, b, j, 0)),
            pl.BlockSpec((None, None, cpb, vcache.shape[3], KC), lambda b, j: (l, b, j, 0, 0)),
            pl.BlockSpec((tq, h_idx * d_idx), lambda b, j: (b, 0)),
            pl.BlockSpec((tq, d_idx), lambda b, j: (b, 0)),
            pl.BlockSpec((None, None, past, d_idx), lambda b, j: (l, b, 0, 0)),
            pl.BlockSpec((None, LANES, tq), lambda b, j: (b, 0, 0)),
        ],
        out_specs=pl.BlockSpec((tq, C), lambda b, j: (b, 0)),
        out_shape=jax.ShapeDtypeStruct((n_batch * tq, C), BF),
        scratch_shapes=[
            pltpu.VMEM((past // KC + 1, KC, tq), I16), pltpu.VMEM((past // KC + 1, KC, tq), I16),
            pltpu.VMEM((1, tq), I32),
            pltpu.VMEM((C, tq), BF), pltpu.VMEM((SUBLANES, tq), F32),
            pltpu.VMEM((n_heads, 1, tq), F32), pltpu.VMEM((n_heads, 1, tq), F32),
            pltpu.VMEM((n_heads, hd, tq), F32),
            pltpu.VMEM((n_heads, KC, tq), F32), pltpu.VMEM((n_heads, KC, tq), BF),
            pltpu.VMEM((n_heads, 1, tq), F32),
        ],
        compiler_params=_cparams(("parallel", "arbitrary")),
        name="dsa_sample",
    )(proj, proj, proj, kcache, vcache, qi, ki, kicache, smt)


def _split3(x):
    hi = x.astype(BF)
    r = x - hi.astype(F32)
    mid = r.astype(BF)
    lo = (r - mid.astype(F32)).astype(BF)
    return hi, mid, lo


def _gdn_kernel(x_ref, gb_ref, sm_ref, cp_ref, s0_ref, cw_ref, ng_ref, ob_ref, sn_ref, S_ref, cbuf_ref,
                *, n_heads, hd, t_valid, conv_w, b_off, a_off, nchunks):
    c = pl.program_id(1)
    C = x_ref.shape[0]
    cd = n_heads * hd
    halo = conv_w - 1

    @pl.when(c == 0)
    def _():
        S_ref[...] = s0_ref[...]
        cbuf_ref[SUBLANES - halo:SUBLANES, :] = cp_ref[...]

    cbuf_ref[SUBLANES:SUBLANES + C, :] = x_ref[...]
    conv = None
    for i in range(conv_w):
        term = cbuf_ref[SUBLANES - halo + i:SUBLANES - halo + i + C, :] * cw_ref[i:i + 1, :]
        conv = term if conv is None else conv + term
    cbuf_ref[0:SUBLANES, :] = cbuf_ref[C:C + SUBLANES, :]
    act = conv * _sigmoid(conv)

    sm = sm_ref[...]
    row = c * C + lax.broadcasted_iota(I32, (C, LANES), 0)
    lane = lax.broadcasted_iota(I32, (C, LANES), 1)
    valid = row < t_valid
    la = jnp.where(valid & (lane >= a_off) & (lane < a_off + n_heads), sm, 0.0)
    beta_t = jnp.where(valid, sm, 0.0)
    ri = lax.broadcasted_iota(I32, (C, C), 0)
    ci = lax.broadcasted_iota(I32, (C, C), 1)
    causal = ri >= ci
    strict = ri > ci
    ltri = jnp.where(causal, 1.0, 0.0).astype(BF)
    g_t = sum(jnp.dot(ltri, piece, preferred_element_type=F32) for piece in _split3(la))
    g_rows = jnp.concatenate([g_t, jnp.zeros((LANES - C, LANES), F32)], axis=0).T

    H = range(n_heads)
    dot = functools.partial(jnp.dot, preferred_element_type=F32)
    dot_nt = functools.partial(lax.dot_general, dimension_numbers=NT_DIMS, preferred_element_type=F32)
    qh = [act[:, h * hd:(h + 1) * hd] for h in H]
    kh = [act[:, cd + h * hd:cd + (h + 1) * hd] for h in H]
    vh = [act[:, 2 * cd + h * hd:2 * cd + (h + 1) * hd] for h in H]
    qn = [qh[h] * lax.rsqrt(jnp.sum(qh[h] * qh[h], axis=-1, keepdims=True) + NORM_EPS) * (hd ** -0.5) for h in H]
    kn = [kh[h] * lax.rsqrt(jnp.sum(kh[h] * kh[h], axis=-1, keepdims=True) + NORM_EPS) for h in H]
    beta = [beta_t[:, b_off + h:b_off + h + 1] for h in H]
    g_c = [g_t[:, a_off + h:a_off + h + 1] for h in H]
    g_r = [g_rows[a_off + h:a_off + h + 1, :C] for h in H]
    g_last = [g_t[C - 1:C, a_off + h:a_off + h + 1] for h in H]
    decay = [jnp.where(causal, jnp.exp(jnp.where(causal, g_c[h] - g_r[h], 0.0)), 0.0) for h in H]
    eg = [jnp.exp(g_c[h]) for h in H]
    kb = [kn[h].astype(BF) for h in H]
    qb = [qn[h].astype(BF) for h in H]
    Sb = [S_ref[h].astype(BF) for h in H]
    kk = [dot_nt(kb[h], kb[h]) for h in H]
    qk = [dot_nt(qb[h], kb[h]) * decay[h] for h in H]
    o_state = [dot((qn[h] * eg[h]).astype(BF), Sb[h]) for h in H]
    N = [jnp.where(strict, -(beta[h] * kk[h] * decay[h]), 0.0) for h in H]
    P = list(N)
    for _ in range(int(np.log2(C)) - 1):
        Pb = [P[h].astype(BF) for h in H]
        P = [dot(Pb[h], Pb[h]) for h in H]
        NP = [dot(N[h].astype(BF), P[h].astype(BF)) for h in H]
        N = [N[h] + P[h] + NP[h] for h in H]
    rhs = [jnp.concatenate([vh[h] * beta[h], kn[h] * (beta[h] * eg[h])], axis=1) for h in H]
    corr = [dot(N[h].astype(BF), rhs[h].astype(BF)) for h in H]
    sol = [rhs[h] + corr[h] for h in H]
    su = [dot(sol[h][:, hd:].astype(BF), Sb[h]) for h in H]
    ub = [(sol[h][:, :hd] - su[h]).astype(BF) for h in H]
    o = [o_state[h] + dot(qk[h].astype(BF), ub[h]) for h in H]
    kd = [(kn[h] * jnp.exp(g_last[h] - g_c[h])).astype(BF) for h in H]
    upd = [lax.dot_general(kd[h], ub[h], TN_DIMS, preferred_element_type=F32) for h in H]
    for h in H:
        cols = slice(h * hd, (h + 1) * hd)
        S_ref[h] = S_ref[h] * jnp.exp(g_last[h]) + upd[h]
        gh = gb_ref[:, cols]
        on = o[h] * lax.rsqrt(jnp.mean(o[h] * o[h], axis=-1, keepdims=True) + NORM_EPS) * ng_ref[...]
        ob_ref[:, cols] = (on * (gh * _sigmoid(gh))).astype(BF)

    @pl.when(c == nchunks - 1)
    def _():
        sn_ref[...] = S_ref[...]


def gdn(proj, sm, conv_prev, s0, cw, ng, *, n_batch, rows_per_batch, t_valid, n_heads, hd, col0, b_off, a_off):
    cd = n_heads * hd
    nchunks = rows_per_batch // DN_C
    conv_w = cw.shape[0]
    qkv_blk = col0 // (3 * cd)
    g_blk = (col0 + 3 * cd) // cd
    return pl.pallas_call(
        functools.partial(_gdn_kernel, n_heads=n_heads, hd=hd, t_valid=t_valid, conv_w=conv_w,
                          b_off=b_off, a_off=a_off, nchunks=nchunks),
        grid=(n_batch, nchunks),
        in_specs=[
            pl.BlockSpec((DN_C, 3 * cd), lambda b, c: (b * nchunks + c, qkv_blk)),
            pl.BlockSpec((DN_C, cd), lambda b, c: (b * nchunks + c, g_blk)),
            pl.BlockSpec((DN_C, LANES), lambda b, c: (b * nchunks + c, 0)),
            pl.BlockSpec((None, conv_w - 1, 3 * cd), lambda b, c: (b, 0, 0)),
            pl.BlockSpec((None, n_heads, hd, hd), lambda b, c: (b, 0, 0, 0)),
            pl.BlockSpec((conv_w, 3 * cd), lambda b, c: (0, 0)),
            pl.BlockSpec((1, hd), lambda b, c: (0, 0)),
        ],
        out_specs=[pl.BlockSpec((DN_C, cd), lambda b, c: (b * nchunks + c, 0)),
                   pl.BlockSpec((None, n_heads, hd, hd), lambda b, c: (b, 0, 0, 0))],
        out_shape=[jax.ShapeDtypeStruct((n_batch * rows_per_batch, cd), BF),
                   jax.ShapeDtypeStruct((n_batch, n_heads, hd, hd), F32)],
        scratch_shapes=[pltpu.VMEM((n_heads, hd, hd), F32),
                        pltpu.VMEM((DN_C + SUBLANES, 3 * cd), F32)],
        compiler_params=_cparams(("parallel", "arbitrary")),
        name="gdn",
    )(proj, proj, sm, conv_prev, s0, cw, ng)


def _merge_kernel(h_ref, hb_ref, oa_ref, ob_ref, wga_ref, wgb_ref, woa_ref, wob_ref, wout_ref, g_ref, b_ref,
                  o_ref, acc_ref, *, alpha, nn, rs):
    j = pl.program_id(1)

    @pl.when(j == 0)
    def _():
        acc_ref[...] = jnp.zeros_like(acc_ref)

    hb = hb_ref[...]
    ga = _sigmoid(jnp.dot(hb, wga_ref[...], preferred_element_type=F32))
    gb = _sigmoid(jnp.dot(hb, wgb_ref[...], preferred_element_type=F32))
    y = (ga * jnp.dot(oa_ref[...], woa_ref[...], preferred_element_type=F32)
         + gb * jnp.dot(ob_ref[...], wob_ref[...], preferred_element_type=F32))
    acc_ref[...] += jnp.dot(y.astype(BF), wout_ref[...], preferred_element_type=F32)

    @pl.when(j == nn - 1)
    def _():
        def slab(r, carry):
            rows = pl.ds(pl.multiple_of(r * rs, rs), rs)
            o_ref[rows, :] = _layer_norm(alpha * h_ref[rows, :] + acc_ref[rows, :], g_ref[...], b_ref[...])
            return carry
        lax.fori_loop(0, h_ref.shape[0] // rs, slab, 0)


def merge_ln(h, hb, oa, ob, wgate, woa, wob, wout, g, b, l, alpha, tm, tn):
    M, D = h.shape
    ca, cb = oa.shape[1], ob.shape[1]
    nn = D // tn
    rs = 64 if tm % 64 == 0 else 16
    return pl.pallas_call(
        functools.partial(_merge_kernel, alpha=alpha, nn=nn, rs=rs),
        grid=(M // tm, nn),
        in_specs=[
            pl.BlockSpec((tm, D), lambda i, j: (i, 0)),
            pl.BlockSpec((tm, D), lambda i, j: (i, 0)),
            pl.BlockSpec((tm, ca), lambda i, j: (i, 0)),
            pl.BlockSpec((tm, cb), lambda i, j: (i, 0)),
            pl.BlockSpec((None, D, tn), lambda i, j: (l, 0, j)),
            pl.BlockSpec((None, D, tn), lambda i, j: (l, 0, nn + j)),
            pl.BlockSpec((None, ca, tn), lambda i, j: (l, 0, j)),
            pl.BlockSpec((None, cb, tn), lambda i, j: (l, 0, j)),
            pl.BlockSpec((None, tn, D), lambda i, j: (l, j, 0)),
            pl.BlockSpec((1, D), lambda i, j: (0, 0)),
            pl.BlockSpec((1, D), lambda i, j: (0, 0)),
        ],
        out_specs=pl.BlockSpec((tm, D), lambda i, j: (i, 0)),
        out_shape=jax.ShapeDtypeStruct((M, D), F32),
        scratch_shapes=[pltpu.VMEM((tm, D), F32)],
        compiler_params=_cparams(("parallel", "arbitrary")),
        name="merge_ln",
    )(h, hb, oa, ob, wgate, wgate, woa, wob, wout, g, b)


def _round_up(n, m):
    return -(-n // m) * m


def kernel(x_prompt, x_sample, cache_k, cache_v, cache_kidx, state_conv, state_dn, meta, w_ff_gu, w_ff_down, ln_g, ln_b, w_in, conv_w, idx_ln_g, idx_ln_b, a_log, dt_bias, dn_norm_g, w_oa, w_ob, w_out):
    depth = w_in.shape[0]
    B, S, D = x_prompt.shape
    Bs, Ts, _ = x_sample.shape
    n_meta = meta.shape[0]
    past, HA, HDA = cache_k.shape[2], cache_k.shape[3], cache_k.shape[4]
    CA = HA * HDA
    d_idx = cache_kidx.shape[-1]
    HB, HDB = state_dn.shape[2], state_dn.shape[3]
    CD = HB * HDB
    cw_len = conv_w.shape[1]
    F = w_ff_down.shape[2]
    n_in = w_in.shape[2]
    h_idx = (n_in - 3 * CA - d_idx - 4 * CD - 2 * HB - 2 * D) // (d_idx + 1)
    assert B == 1 and Ts == DN_C and CHUNK == 64 and n_meta <= CHUNK
    assert d_idx + h_idx + 2 * HB <= LANES and h_idx <= SUBLANES and HA * HDA == CA
    assert past % KC == 0 and CA == CD
    alpha = float((2 * depth) ** 0.25)
    T = n_meta + S
    Mp = _round_up(T, ROW_PAD)
    n_sel_p = min(MAX_SEL, S // 4)
    n_sel_s = min(MAX_SEL, (past + Ts) // 4)
    w_off, b_off, a_off = d_idx, d_idx + h_idx, d_idx + h_idx + HB

    Fp = _round_up(F, FF_ALIGN)
    padf = ((0, 0), (0, 0), (0, 0), (0, Fp - F))
    wg = jnp.pad(w_ff_gu[..., :F].astype(BF), padf)
    wu = jnp.pad(w_ff_gu[..., F:].astype(BF), padf)
    wd = jnp.pad(w_ff_down.astype(BF), ((0, 0), (0, 0), (0, Fp - F), (0, 0)))
    offs = np.cumsum([0, CA, CA, CA, h_idx * d_idx, d_idx, h_idx, 3 * CD, CD, HB, HB, D, D])
    span = lambda i, j: w_in[:, :, offs[i]:offs[j]].astype(BF)
    w_main = jnp.concatenate([span(0, 3), span(6, 8)], axis=2)
    pad_small = LANES - (d_idx + h_idx + 2 * HB)
    w_idx = jnp.concatenate([span(3, 6), span(8, 10), jnp.zeros((depth, D, pad_small), BF)], axis=2)
    w_gate = span(10, 12)
    woa, wob, wout = w_oa.astype(BF), w_ob.astype(BF), w_out.astype(BF)

    def lane_vec(v, off):
        return jnp.zeros((depth, 1, LANES), F32).at[:, 0, off:off + v.shape[1]].set(v)
    lng_v, lnb_v = lane_vec(idx_ln_g, 0), lane_vec(idx_ln_b, 0)
    alog_v, dtb_v = lane_vec(a_log, a_off), lane_vec(dt_bias, a_off)

    xp = jnp.concatenate([meta.astype(F32), x_prompt[0], jnp.zeros((Mp - T, D), F32)], axis=0)
    xs = x_sample.reshape(Bs * Ts, D)
    Ms = Bs * Ts
    tm_p, tm_s = _pick_tile(Mp, TM_FFN), _pick_tile(Ms, 512)
    tmm_p, tmm_s = _pick_tile(Mp, TM_MM), _pick_tile(Ms, 1024)
    tf = FF_ALIGN
    tn_mm = _pick_tile(w_main.shape[2], TN_MM)
    tn_mg = _pick_tile(D, TN_MERGE)
    tq = _pick_tile(Mp, TQ)
    tk_p = _pick_tile(Mp, TK_PROMPT)
    tk_c = _pick_tile(past, TK_CACHE)
    assert tq % LANES == 0 and tk_p % KC == 0 and tk_c % KC == 0 and Mp % KC == 0

    kcache = cache_k.astype(BF).reshape(depth, Bs, past, CA)
    vcache = cache_v.astype(BF).reshape(depth, Bs, past // KC, KC, HA, HDA).transpose(0, 1, 2, 4, 5, 3)
    vcache = jnp.concatenate([vcache, jnp.ones((depth, Bs, past // KC, HA, ONES_ROWS, KC), BF)], axis=4)
    vcache = vcache.reshape(depth, Bs, past // KC, HA * (HDA + ONES_ROWS), KC)
    kicache = cache_kidx.astype(BF)
    conv0 = jnp.zeros((1, cw_len - 1, 3 * CD), F32)
    s0 = jnp.zeros((1, HB, HDB, HDB), F32)

    outs = [[] for _ in range(10)]
    for l in range(depth):
        g3 = [ln_g[l, i][None] for i in range(3)]
        b3 = [ln_b[l, i][None] for i in range(3)]
        ng = dn_norm_g[l][None]
        streams = []
        for (x, is_prompt) in ((xp, True), (xs, False)):
            tm, tmm = (tm_p, tmm_p) if is_prompt else (tm_s, tmm_s)
            h, hb = ffn_ln(x, wg, wu, wd, g3[0], b3[0], l, 0, alpha, tm, tf)
            proj = matmul(hb, w_main, l, tmm, tn_mm)
            qi, ki, sm = idx_proj(hb, w_idx, lng_v[l], lnb_v[l], alog_v[l], dtb_v[l], l, tmm, d_idx, h_idx, HB)
            if is_prompt:
                kb16 = proj[:, CA:2 * CA].astype(BF)
                vt4 = proj[:, 2 * CA:3 * CA].astype(BF).reshape(Mp // KC, KC, HA, HDA).transpose(0, 2, 3, 1)
                vt3 = jnp.concatenate([vt4, jnp.ones((Mp // KC, HA, ONES_ROWS, KC), BF)], axis=2)
                vt3 = vt3.reshape(Mp // KC, HA * (HDA + ONES_ROWS), KC)
                oa = dsa_prompt(proj, kb16, vt3, qi, ki, sm.T, t_valid=T, n_meta=n_meta, n_heads=HA, hd=HDA,
                                h_idx=h_idx, d_idx=d_idx, n_sel=n_sel_p, tq=tq, tk=tk_p)
                ob, s_new = gdn(proj, sm, conv0, s0, conv_w[l], ng, n_batch=1, rows_per_batch=Mp, t_valid=T,
                                n_heads=HB, hd=HDB, col0=3 * CA, b_off=b_off, a_off=a_off)
                nrow = T
            else:
                smt = sm.reshape(Bs, Ts, LANES).swapaxes(1, 2)
                oa = dsa_sample(proj, kcache, vcache, qi, ki, kicache, smt, l, n_batch=Bs, t_new=Ts,
                                n_heads=HA, hd=HDA, h_idx=h_idx, d_idx=d_idx, n_sel=n_sel_s, tk=tk_c)
                ob, s_new = gdn(proj, sm, state_conv[l], state_dn[l], conv_w[l], ng, n_batch=Bs, rows_per_batch=Ts,
                                t_valid=Ts, n_heads=HB, hd=HDB, col0=3 * CA, b_off=b_off, a_off=a_off)
                nrow = Ms
            x = merge_ln(h, hb, oa, ob, w_gate, woa, wob, wout, g3[1], b3[1], l, alpha, tm, tn_mg)
            x, _ = ffn_ln(x, wg, wu, wd, g3[2], b3[2], l, 1, alpha, tm, tf)
            nb = 1 if is_prompt else Bs
            per = nrow // nb
            k_new = proj[:nrow, CA:2 * CA].reshape(nb, per, HA, HDA)
            v_new = proj[:nrow, 2 * CA:3 * CA].reshape(nb, per, HA, HDA)
            ki_new = sm[:nrow, :d_idx].reshape(nb, per, d_idx)
            qkv_b = proj[:nrow, 3 * CA:3 * CA + 3 * CD].reshape(nb, per, 3 * CD)
            if is_prompt:
                conv_new = qkv_b[:, per - (cw_len - 1):]
            else:
                conv_new = jnp.concatenate([state_conv[l], qkv_b], axis=1)[:, -(cw_len - 1):]
            streams.append((x, k_new, v_new, ki_new, conv_new, s_new))
        (xp, *st_p), (xs, *st_s) = streams
        for i, a in enumerate(st_p):
            outs[i].append(a)
        for i, a in enumerate(st_s):
            outs[5 + i].append(a)
    y_prompt = xp[n_meta:T][None]
    y_sample = xs.reshape(Bs, Ts, D)
    return (y_prompt, y_sample) + tuple(jnp.stack(o) for o in outs)
```

```python
import functools

import numpy as np
import jax
import jax.numpy as jnp
from jax import lax
from jax.experimental import pallas as pl
from jax.experimental.pallas import tpu as pltpu

CHUNK = 64
MAX_SEL = 256
DN_C = 64
LN_EPS = 1e-5
NORM_EPS = 1e-6

LANES = 128
SUBLANES = 8
KC = 256
FF_ALIGN = 512
VMEM_LIMIT = 56 << 20

ROW_PAD = 1280
TM_FFN = 640
TM_MM = 1280
TN_MM = 1024
TN_MERGE = 512
TQ = 256
TK_PROMPT = 1280
TK_CACHE = 1024

BF = jnp.bfloat16
F32 = jnp.float32
I32 = jnp.int32
I16 = jnp.int16
I16_OFF = 2 ** 15
PACK16 = 16
INT_MIN = -(2 ** 31)
NEG = -1e30
LOG2E = 1.4426950408889634
ONES_ROWS = 16
NT_DIMS = (((1,), (1,)), ((), ()))
TN_DIMS = (((0,), (0,)), ((), ()))


def _cparams(sem):
    return pltpu.CompilerParams(dimension_semantics=sem, vmem_limit_bytes=VMEM_LIMIT)


def _sigmoid(x):
    return 1.0 / (1.0 + jnp.exp(-x))


def _layer_norm(y, g, b):
    mu = jnp.mean(y, axis=-1, keepdims=True)
    d = y - mu
    var = jnp.mean(d * d, axis=-1, keepdims=True)
    return d * lax.rsqrt(var + LN_EPS) * g + b


def _pick_tile(n, pref):
    if n <= pref:
        return n
    for t in range(pref, 15, -16):
        if n % t == 0:
            return t
    return n


def _ffn_kernel(x_ref, wg_ref, wu_ref, wd_ref, g_ref, b_ref, o_ref, ob_ref, xb_ref, acc_ref, *, alpha, nf, rs):
    j = pl.program_id(1)

    @pl.when(j == 0)
    def _():
        xb_ref[...] = x_ref[...].astype(BF)
        acc_ref[...] = jnp.zeros_like(acc_ref)

    xb = xb_ref[...]
    hg = jnp.dot(xb, wg_ref[...], preferred_element_type=F32)
    hu = jnp.dot(xb, wu_ref[...], preferred_element_type=F32)
    a = (hg * _sigmoid(hg) * hu).astype(BF)
    acc_ref[...] += jnp.dot(a, wd_ref[...], preferred_element_type=F32)

    @pl.when(j == nf - 1)
    def _():
        def slab(r, carry):
            rows = pl.ds(pl.multiple_of(r * rs, rs), rs)
            y = alpha * x_ref[rows, :] + 0.5 * acc_ref[rows, :]
            z = _layer_norm(y, g_ref[...], b_ref[...])
            o_ref[rows, :] = z
            ob_ref[rows, :] = z.astype(BF)
            return carry
        lax.fori_loop(0, x_ref.shape[0] // rs, slab, 0)


def ffn_ln(x, wg, wu, wd, g, b, l, s, alpha, tm, tf):
    M, D = x.shape
    nf = wd.shape[2] // tf
    rs = 64 if tm % 64 == 0 else 16
    return pl.pallas_call(
        functools.partial(_ffn_kernel, alpha=alpha, nf=nf, rs=rs),
        grid=(M // tm, nf),
        in_specs=[
            pl.BlockSpec((tm, D), lambda i, j: (i, 0)),
            pl.BlockSpec((None, None, D, tf), lambda i, j: (l, s, 0, j)),
            pl.BlockSpec((None, None, D, tf), lambda i, j: (l, s, 0, j)),
            pl.BlockSpec((None, None, tf, D), lambda i, j: (l, s, j, 0)),
            pl.BlockSpec((1, D), lambda i, j: (0, 0)),
            pl.BlockSpec((1, D), lambda i, j: (0, 0)),
        ],
        out_specs=[pl.BlockSpec((tm, D), lambda i, j: (i, 0)), pl.BlockSpec((tm, D), lambda i, j: (i, 0))],
        out_shape=[jax.ShapeDtypeStruct((M, D), F32), jax.ShapeDtypeStruct((M, D), BF)],
        scratch_shapes=[pltpu.VMEM((tm, D), BF), pltpu.VMEM((tm, D), F32)],
        compiler_params=_cparams(("parallel", "arbitrary")),
        name="ffn_ln",
    )(x, wg, wu, wd, g, b)


def _mm_kernel(x_ref, w_ref, o_ref):
    o_ref[...] = jnp.dot(x_ref[...], w_ref[...], preferred_element_type=F32)


def matmul(xb, w, l, tm, tn):
    M, D = xb.shape
    N = w.shape[2]
    return pl.pallas_call(
        _mm_kernel,
        grid=(M // tm, N // tn),
        in_specs=[pl.BlockSpec((tm, D), lambda i, j: (i, 0)),
                  pl.BlockSpec((None, D, tn), lambda i, j: (l, 0, j))],
        out_specs=pl.BlockSpec((tm, tn), lambda i, j: (i, j)),
        out_shape=jax.ShapeDtypeStruct((M, N), F32),
        compiler_params=_cparams(("parallel", "arbitrary")),
        name="in_proj",
    )(xb, w)


def _idx_kernel(x_ref, w_ref, lng_ref, lnb_ref, alog_ref, dtb_ref, qi_ref, ki_ref, sm_ref, *, n_qi, d_idx, h_idx, h_b):
    r = jnp.dot(x_ref[...], w_ref[...], preferred_element_type=F32)
    qi_ref[...] = r[:, :n_qi].astype(BF)
    t = r[:, n_qi:]
    lane = lax.broadcasted_iota(I32, t.shape, 1)
    isk = lane < d_idx
    mu = jnp.sum(jnp.where(isk, t, 0.0), axis=-1, keepdims=True) / d_idx
    d = jnp.where(isk, t - mu, 0.0)
    var = jnp.sum(d * d, axis=-1, keepdims=True) / d_idx
    kn = d * lax.rsqrt(var + LN_EPS) * lng_ref[...] + lnb_ref[...]
    wv = t * (h_idx ** -0.5)
    beta = _sigmoid(t)
    z = t + dtb_ref[...]
    softplus = jnp.maximum(z, 0.0) + jnp.log(1.0 + jnp.exp(-jnp.abs(z)))
    la = -jnp.exp(alog_ref[...]) * softplus
    w_off, b_off, a_off = d_idx, d_idx + h_idx, d_idx + h_idx + h_b
    out = jnp.where(isk, kn,
                    jnp.where(lane < b_off, wv,
                              jnp.where(lane < a_off, beta,
                                        jnp.where(lane < a_off + h_b, la, 0.0))))
    sm_ref[...] = out
    ki_ref[...] = kn[:, :d_idx].astype(BF)


def idx_proj(xb, w, lng, lnb, alog, dtb, l, tm, d_idx, h_idx, h_b):
    M, D = xb.shape
    N = w.shape[2]
    n_qi = N - LANES
    vec = pl.BlockSpec((1, LANES), lambda i: (0, 0))
    return pl.pallas_call(
        functools.partial(_idx_kernel, n_qi=n_qi, d_idx=d_idx, h_idx=h_idx, h_b=h_b),
        grid=(M // tm,),
        in_specs=[pl.BlockSpec((tm, D), lambda i: (i, 0)),
                  pl.BlockSpec((None, D, N), lambda i: (l, 0, 0)),
                  vec, vec, vec, vec],
        out_specs=[pl.BlockSpec((tm, n_qi), lambda i: (i, 0)),
                   pl.BlockSpec((tm, d_idx), lambda i: (i, 0)),
                   pl.BlockSpec((tm, LANES), lambda i: (i, 0))],
        out_shape=[jax.ShapeDtypeStruct((M, n_qi), BF),
                   jax.ShapeDtypeStruct((M, d_idx), BF),
                   jax.ShapeDtypeStruct((M, LANES), F32)],
        compiler_params=_cparams(("parallel",)),
        name="idx_proj",
    )(xb, w, lng, lnb, alog, dtb)


def _sort_key(s):
    b = pltpu.bitcast(s + 0.0, I32)
    return b ^ ((b >> 31) & 0x7FFFFFFF)


def _transpose_to(dst_ref, x):
    tq = x.shape[0]
    if tq % LANES:
        x = jnp.concatenate([x, jnp.zeros((LANES - tq % LANES, x.shape[1]), F32)], axis=0)
    dst_ref[...] = x.T[:, :tq].astype(dst_ref.dtype)


def _index_scores(ki_chunk, qit_ref, w_rows, h_idx, d_idx):
    s = [jnp.dot(ki_chunk, qit_ref[h * d_idx:(h + 1) * d_idx, :], preferred_element_type=F32)
         for h in range(h_idx)]
    acc = w_rows[0:1, :] * jnp.maximum(s[0], 0.0)
    for h in range(1, h_idx):
        acc = acc + w_rows[h:h + 1, :] * jnp.maximum(s[h], 0.0)
    return acc


def _store_key(hi_ref, lo_ref, where, key):
    hi_ref[where] = (key >> 16).astype(I16)
    lo_ref[where] = ((key & 0xFFFF) - I16_OFF).astype(I16)


def _count16(nck, tq, pred):
    def body(i, cnt):
        ones = (jnp.where(pred(2 * i), jnp.int16(1), jnp.int16(0))
                + jnp.where(pred(2 * i + 1), jnp.int16(1), jnp.int16(0)))
        parts = [ones[r * PACK16:(r + 1) * PACK16] for r in range(ones.shape[0] // PACK16)]
        while len(parts) > 1:
            parts = [a + b for a, b in zip(parts[::2], parts[1::2])]
        return cnt + parts[0]
    cnt16 = lax.fori_loop(0, (nck + 1) // 2, body, jnp.zeros((PACK16, tq), I16))
    return cnt16.astype(I32).sum(axis=0, keepdims=True)


def _mask_pad_chunk(hi_ref, lo_ref, nck):
    if isinstance(nck, int):
        if nck % 2:
            hi_ref[nck] = jnp.full(hi_ref.shape[1:], -I16_OFF, I16)
            lo_ref[nck] = jnp.full(lo_ref.shape[1:], -I16_OFF, I16)
        return

    @pl.when(nck % 2 == 1)
    def _():
        hi_ref[nck] = jnp.full(hi_ref.shape[1:], -I16_OFF, I16)
        lo_ref[nck] = jnp.full(lo_ref.shape[1:], -I16_OFF, I16)


def _search16(nck, tq, n_sel, base, cnt0, count_ge):
    def bit_body(i, carry):
        thr, cnt_thr = carry
        cand = thr + jnp.left_shift(jnp.int32(1), 15 - i)
        cnt = base + count_ge(cand.astype(I16))
        ok = cnt >= n_sel
        return jnp.where(ok, cand, thr), jnp.where(ok, cnt, cnt_thr)
    return lax.fori_loop(0, 16, bit_body, (jnp.full((1, tq), -I16_OFF, I32), cnt0))


def _select_top(hi_ref, lo_ref, nck, n_sel, tq, idx_bits, cut_ref):
    kc = hi_ref.shape[1]
    zero = jnp.zeros((1, tq), I32)
    _mask_pad_chunk(hi_ref, lo_ref, nck)
    thr_hi, cnt_ge_hi = _search16(nck, tq, n_sel, zero, zero,
                                  lambda t: _count16(nck, tq, lambda c: hi_ref[c] >= t))
    t_hi = thr_hi.astype(I16)
    cnt_gt_hi = _count16(nck, tq, lambda c: hi_ref[c] > t_hi)

    def lo_body(c, carry):
        lo_ref[c] = jnp.where(hi_ref[c] == t_hi, lo_ref[c], jnp.int16(-I16_OFF))
        return carry
    lax.fori_loop(0, nck, lo_body, 0)
    thr_lo, cnt_thr = _search16(nck, tq, n_sel, cnt_gt_hi, cnt_ge_hi,
                                lambda t: _count16(nck, tq, lambda c: lo_ref[c] >= t))
    t_lo = thr_lo.astype(I16)
    has_tie = (thr_hi > -I16_OFF) & (cnt_thr > n_sel)
    cut_ref[...] = jnp.full((1, tq), I16_OFF - 1, I32)

    def kidx16(c):
        return (c * kc + lax.broadcasted_iota(I32, (kc, tq), 0)).astype(I16)

    @pl.when(jnp.max(has_tie.astype(I32)) > 0)
    def _():
        need = n_sel - cnt_gt_hi - _count16(nck, tq, lambda c: (hi_ref[c] == t_hi) & (lo_ref[c] > t_lo))

        def idx_body(i, cut):
            cand = cut + jnp.left_shift(jnp.int32(1), idx_bits - 1 - i)
            c16 = cand.astype(I16)
            cnt = _count16(nck, tq, lambda c: (hi_ref[c] == t_hi) & (lo_ref[c] == t_lo) & (kidx16(c) < c16))
            return jnp.where(cnt < need, cand, cut)

        cut = lax.fori_loop(0, idx_bits, idx_body, zero)
        cut_ref[...] = jnp.where(has_tie, cut, I16_OFF - 1)

    cut = cut_ref[...].astype(I16)

    def bias_body(c, carry):
        hi, lo = hi_ref[c], lo_ref[c]
        sel = (hi > t_hi) | ((hi == t_hi) & ((lo > t_lo) | ((lo == t_lo) & (kidx16(c) <= cut))))
        sel = sel & (hi > jnp.int16(-I16_OFF))
        hi_ref[c] = pltpu.bitcast(jnp.where(sel, jnp.zeros((), BF), jnp.full((), NEG, BF)), I16)
        return carry
    lax.fori_loop(0, nck, bias_body, 0)


def _flash_scores(slot, bias16, dist, k_heads, row_sum, qt_ref, m_ref, l_ref, lt_ref, p_ref, a_ref, n_heads, hd):
    kc = dist.shape[0]
    bias = pltpu.bitcast(bias16, BF).astype(F32)
    for h in range(n_heads):
        cols = slice(h * hd, (h + 1) * hd)
        lt_ref[h, 0:kc, :] = jnp.dot(k_heads[h](), qt_ref[cols, :], preferred_element_type=F32)
    for h in range(n_heads):
        slope = float(2.0 ** (-(8.0 / n_heads) * (h + 1)) * LOG2E)
        lt = lt_ref[h, 0:kc, :] + (bias - slope * dist)
        m_old = m_ref[h]
        m_new = jnp.maximum(m_old, jnp.max(lt, axis=0, keepdims=True))
        p = jnp.exp2(lt - m_new)
        a = jnp.exp2(m_old - m_new)
        if row_sum:
            l_ref[h] = a * l_ref[h] + jnp.sum(p, axis=0, keepdims=True)
        p_ref[slot, h, 0:kc, :] = p.astype(BF)
        a_ref[slot, h] = a
        m_ref[h] = m_new


def _flash_values(slot, kc, vt_heads, l_ref, acc_ref, p_ref, a_ref, n_heads, hd):
    for h in range(n_heads):
        pv = jnp.dot(vt_heads[h](), p_ref[slot, h, 0:kc, :], preferred_element_type=F32)
        a = a_ref[slot, h]
        acc_ref[h] = a * acc_ref[h] + pv[:hd]
        if pv.shape[0] > hd:
            l_ref[h] = a * l_ref[h] + pv[hd:hd + 1]


def _flash_init(m_ref, l_ref, acc_ref):
    m_ref[...] = jnp.full(m_ref.shape, NEG, F32)
    l_ref[...] = jnp.zeros(l_ref.shape, F32)
    acc_ref[...] = jnp.zeros(acc_ref.shape, F32)


def _flash_finish(o_ref, l_ref, acc_ref, n_heads, hd, tq):
    for h in range(n_heads):
        ot = acc_ref[h] / l_ref[h]
        if tq % LANES:
            ot = jnp.concatenate([ot, jnp.zeros((hd, LANES - tq % LANES), F32)], axis=1)
        o_ref[:, h * hd:(h + 1) * hd] = ot.T[:tq].astype(BF)


def _dsa_prompt_kernel(qmap, kmap, nckq, q_ref, k_ref, vt_ref, qi_ref, ki_ref, smt_ref, o_ref,
                       hi_ref, lo_ref, cut_ref, qt_ref, qit_ref, w_ref, m_ref, l_ref, acc_ref, lt_ref, p_ref, a_ref,
                       *, n_heads, hd, h_idx, d_idx, n_meta, t_valid, n_sel, cpb, idx_bits):
    p = pl.program_id(0)
    qb, kb = qmap[p], kmap[p]
    nck = nckq[qb]
    tq = q_ref.shape[0]
    qpos = qb * tq + lax.broadcasted_iota(I32, (KC, tq), 1)

    @pl.when(kb == 0)
    def _():
        _transpose_to(qt_ref, q_ref[...] * (hd ** -0.5 * LOG2E))
        _transpose_to(qit_ref, qi_ref[...].astype(F32))
        w_ref[...] = smt_ref[d_idx:d_idx + SUBLANES, :] * (d_idx ** -0.5)
        qchk = jnp.maximum((qpos - n_meta) >> 6, -1)

        def score_body(c, carry):
            k0 = pl.multiple_of(c * KC, KC)
            sc = _index_scores(ki_ref[pl.ds(k0, KC), :], qit_ref, w_ref[...], h_idx, d_idx)
            kpos = k0 + lax.broadcasted_iota(I32, (KC, tq), 0)
            allowed = (jnp.maximum((kpos - n_meta) >> 6, -1) <= qchk) & (kpos < t_valid)
            _store_key(hi_ref, lo_ref, c, jnp.where(allowed, _sort_key(sc), INT_MIN))
            return carry
        lax.fori_loop(0, nck, score_body, 0)
        _select_top(hi_ref, lo_ref, nck, n_sel, tq, idx_bits, cut_ref)
        _flash_init(m_ref, l_ref, acc_ref)

    vr = vt_ref.shape[1] // n_heads

    def chunk_body(c, carry):
        cg = kb * cpb + c
        rows = pl.ds(pl.multiple_of(c * KC, KC), KC)
        kpos = cg * KC + lax.broadcasted_iota(I32, (KC, tq), 0)
        dist = jnp.abs(qpos - kpos).astype(F32)
        k_heads = [functools.partial(lambda h: k_ref[rows, h * hd:(h + 1) * hd], h) for h in range(n_heads)]
        vt_heads = [functools.partial(lambda h: vt_ref[c, h * vr:(h + 1) * vr, :], h) for h in range(n_heads)]
        _flash_scores(0, hi_ref[cg], dist, k_heads, False, qt_ref, m_ref, l_ref, lt_ref, p_ref, a_ref, n_heads, hd)
        _flash_values(0, KC, vt_heads, l_ref, acc_ref, p_ref, a_ref, n_heads, hd)
        return carry
    lax.fori_loop(0, jnp.minimum(cpb, nck - kb * cpb), chunk_body, 0)

    @pl.when((kb + 1) * cpb >= nck)
    def _():
        _flash_finish(o_ref, l_ref, acc_ref, n_heads, hd, tq)


def dsa_prompt(proj, kb16, vt3, qi, ki, smt, *, t_valid, n_meta, n_heads, hd, h_idx, d_idx, n_sel, tq, tk):
    Mp = proj.shape[0]
    C = n_heads * hd
    vr = vt3.shape[1]
    cpb = tk // KC
    nq = Mp // tq
    qlast = np.minimum(np.arange(nq) * tq + tq - 1, t_valid - 1)
    kend = np.minimum(t_valid, n_meta + CHUNK * (np.maximum((qlast - n_meta) // CHUNK, -1) + 1))
    nckq = -(-kend // KC)
    pairs = [(i, j) for i in range(nq) for j in range(-(-int(nckq[i]) // cpb))]
    qmap = jnp.asarray([a for a, _ in pairs], I32)
    kmap = jnp.asarray([b for _, b in pairs], I32)
    idx_bits = max(1, int(np.ceil(np.log2(Mp))))
    grid_spec = pltpu.PrefetchScalarGridSpec(
        num_scalar_prefetch=3,
        grid=(len(pairs),),
        in_specs=[
            pl.BlockSpec((tq, C), lambda p, qm, km, nk: (qm[p], 0)),
            pl.BlockSpec((tk, C), lambda p, qm, km, nk: (km[p], 0)),
            pl.BlockSpec((cpb, vr, KC), lambda p, qm, km, nk: (km[p], 0, 0)),
            pl.BlockSpec((tq, h_idx * d_idx), lambda p, qm, km, nk: (qm[p], 0)),
            pl.BlockSpec((Mp, d_idx), lambda p, qm, km, nk: (0, 0)),
            pl.BlockSpec((LANES, tq), lambda p, qm, km, nk: (0, qm[p])),
        ],
        out_specs=pl.BlockSpec((tq, C), lambda p, qm, km, nk: (qm[p], 0)),
        scratch_shapes=[
            pltpu.VMEM((Mp // KC + 1, KC, tq), I16), pltpu.VMEM((Mp // KC + 1, KC, tq), I16),
            pltpu.VMEM((1, tq), I32),
            pltpu.VMEM((C, tq), BF), pltpu.VMEM((h_idx * d_idx, tq), BF), pltpu.VMEM((SUBLANES, tq), F32),
            pltpu.VMEM((n_heads, 1, tq), F32), pltpu.VMEM((n_heads, 1, tq), F32),
            pltpu.VMEM((n_heads, hd, tq), F32),
            pltpu.VMEM((n_heads, KC, tq), F32), pltpu.VMEM((2, n_heads, KC, tq), BF),
            pltpu.VMEM((2, n_heads, 1, tq), F32),
        ],
    )
    return pl.pallas_call(
        functools.partial(_dsa_prompt_kernel, n_heads=n_heads, hd=hd, h_idx=h_idx, d_idx=d_idx, n_meta=n_meta,
                          t_valid=t_valid, n_sel=n_sel, cpb=cpb, idx_bits=idx_bits),
        grid_spec=grid_spec,
        out_shape=jax.ShapeDtypeStruct((Mp, C), BF),
        compiler_params=_cparams(("arbitrary",)),
        name="dsa_prompt",
    )(qmap, kmap, jnp.asarray(nckq, I32), proj, kb16, vt3, qi, ki, smt)


def _dsa_sample_kernel(q_ref, kn_ref, vn_ref, kc_ref, vc_ref, qi_ref, kin_ref, kic_ref, smt_ref, o_ref,
                       hi_ref, lo_ref, cut_ref, qt_ref, qit_ref, w_ref, m_ref, l_ref, acc_ref, lt_ref, p_ref, a_ref,
                       *, n_heads, hd, h_idx, d_idx, past, n_sel, cpb, nkb, idx_bits):
    kb = pl.program_id(1)
    tq = q_ref.shape[0]
    ncache = past // KC
    nck = ncache + 1
    qpos = past + lax.broadcasted_iota(I32, (KC, tq), 1)

    @pl.when(kb == 0)
    def _():
        _transpose_to(qt_ref, q_ref[...] * (hd ** -0.5 * LOG2E))
        _transpose_to(qit_ref, qi_ref[...].astype(F32))
        w_ref[...] = smt_ref[d_idx:d_idx + SUBLANES, :] * (d_idx ** -0.5)

        def score_body(c, carry):
            k0 = pl.multiple_of(c * KC, KC)
            sc = _index_scores(kic_ref[pl.ds(k0, KC), :], qit_ref, w_ref[...], h_idx, d_idx)
            _store_key(hi_ref, lo_ref, c, _sort_key(sc))
            return carry
        lax.fori_loop(0, ncache, score_body, 0)
        hi_ref[ncache] = jnp.full((KC, tq), -I16_OFF, I16)
        lo_ref[ncache] = jnp.full((KC, tq), -I16_OFF, I16)
        sc = _index_scores(kin_ref[...], qit_ref, w_ref[...], h_idx, d_idx)
        _store_key(hi_ref, lo_ref, (ncache, slice(0, tq), slice(None)), _sort_key(sc))
        _select_top(hi_ref, lo_ref, nck, n_sel, tq, idx_bits, cut_ref)
        _flash_init(m_ref, l_ref, acc_ref)

    def vt_of(v):
        kc = v.shape[0]
        if kc % LANES:
            v = jnp.concatenate([v, jnp.zeros((LANES - kc % LANES, hd), F32)], axis=0)
        return v.T[:, :kc].astype(BF)

    vall = [None, None]
    for c in range(cpb):
        cg = kb * cpb + c
        kall = pltpu.einshape("khd->hkd", kc_ref[c * KC:(c + 1) * KC])
        kpos = cg * KC + lax.broadcasted_iota(I32, (KC, tq), 0)
        dist = jnp.abs(qpos - kpos).astype(F32)
        k_heads = [functools.partial(lambda h, kall=kall: kall[h].astype(BF), h) for h in range(n_heads)]
        if c > 0:
            vt_heads = [functools.partial(lambda h, v=vall[(c - 1) % 2]: vt_of(v[h]), h) for h in range(n_heads)]
            _flash_values((c - 1) % 2, KC, vt_heads, l_ref, acc_ref, p_ref, a_ref, n_heads, hd)
        vall[c % 2] = pltpu.einshape("khd->hkd", vc_ref[c * KC:(c + 1) * KC])
        _flash_scores(c % 2, hi_ref[cg], dist, k_heads, True, qt_ref, m_ref, l_ref, lt_ref, p_ref, a_ref, n_heads, hd)
    vt_heads = [functools.partial(lambda h, v=vall[(cpb - 1) % 2]: vt_of(v[h]), h) for h in range(n_heads)]
    _flash_values((cpb - 1) % 2, KC, vt_heads, l_ref, acc_ref, p_ref, a_ref, n_heads, hd)

    @pl.when(kb == nkb - 1)
    def _():
        tn = kn_ref.shape[0]
        kpos = past + lax.broadcasted_iota(I32, (tn, tq), 0)
        dist = jnp.abs(past + lax.broadcasted_iota(I32, (tn, tq), 1) - kpos).astype(F32)
        k_heads = [functools.partial(lambda h: kn_ref[:, h * hd:(h + 1) * hd].astype(BF), h) for h in range(n_heads)]
        vt_heads = [functools.partial(lambda h: vt_of(vn_ref[:, h * hd:(h + 1) * hd]), h) for h in range(n_heads)]
        _flash_scores(0, hi_ref[ncache, 0:tn, :], dist, k_heads, True, qt_ref, m_ref, l_ref, lt_ref, p_ref, a_ref,
                      n_heads, hd)
        _flash_values(0, tn, vt_heads, l_ref, acc_ref, p_ref, a_ref, n_heads, hd)
        _flash_finish(o_ref, l_ref, acc_ref, n_heads, hd, tq)


def dsa_sample(proj, kcache, vcache, qi, ki, kicache, smt, l, *, n_batch, t_new, n_heads, hd, h_idx, d_idx, n_sel, tk):
    C = n_heads * hd
    past = kcache.shape[2]
    cpb = tk // KC
    nkb = past // tk
    idx_bits = max(1, int(np.ceil(np.log2(past + KC))))
    tq = t_new
    return pl.pallas_call(
        functools.partial(_dsa_sample_kernel, n_heads=n_heads, hd=hd, h_idx=h_idx, d_idx=d_idx, past=past,
                          n_sel=n_sel, cpb=cpb, nkb=nkb, idx_bits=idx_bits),
        grid=(n_batch, nkb),
        in_specs=[
            pl.BlockSpec((tq, C), lambda b, j: (b, 0)),
            pl.BlockSpec((tq, C), lambda b, j: (b, 1)),
            pl.BlockSpec((tq, C), lambda b, j: (b, 2)),
            pl.BlockSpec((None, None, tk, n_heads, hd), lambda b, j: (l, b, j, 0, 0)),
            pl.BlockSpec((None, None, tk, n_heads, hd), lambda b, j: (l, b, j, 0, 0)),
            pl.BlockSpec((tq, h_idx * d_idx), lambda b, j: (b, 0)),
            pl.BlockSpec((tq, d_idx), lambda b, j: (b, 0)),
            pl.BlockSpec((None, None, past, d_idx), lambda b, j: (l, b, 0, 0)),
            pl.BlockSpec((None, LANES, tq), lambda b, j: (b, 0, 0)),
        ],
        out_specs=pl.BlockSpec((tq, C), lambda b, j: (b, 0)),
        out_shape=jax.ShapeDtypeStruct((n_batch * tq, C), BF),
        scratch_shapes=[
            pltpu.VMEM((past // KC + 2, KC, tq), I16), pltpu.VMEM((past // KC + 2, KC, tq), I16),
            pltpu.VMEM((1, tq), I32),
            pltpu.VMEM((C, tq), BF), pltpu.VMEM((h_idx * d_idx, tq), BF), pltpu.VMEM((SUBLANES, tq), F32),
            pltpu.VMEM((n_heads, 1, tq), F32), pltpu.VMEM((n_heads, 1, tq), F32),
            pltpu.VMEM((n_heads, hd, tq), F32),
            pltpu.VMEM((n_heads, KC, tq), F32), pltpu.VMEM((2, n_heads, KC, tq), BF),
            pltpu.VMEM((2, n_heads, 1, tq), F32),
        ],
        compiler_params=_cparams(("parallel", "arbitrary")),
        name="dsa_sample",
    )(proj, proj, proj, kcache, vcache, qi, ki, kicache, smt)


def _split3(x):
    hi = x.astype(BF)
    r = x - hi.astype(F32)
    mid = r.astype(BF)
    lo = (r - mid.astype(F32)).astype(BF)
    return hi, mid, lo


def _gdn_kernel(x_ref, gb_ref, sm_ref, cp_ref, s0_ref, cw_ref, ng_ref, ob_ref, sn_ref, S_ref, cbuf_ref,
                *, n_heads, hd, t_valid, conv_w, b_off, a_off, nchunks):
    c = pl.program_id(1)
    C = x_ref.shape[0]
    cd = n_heads * hd
    halo = conv_w - 1

    @pl.when(c == 0)
    def _():
        S_ref[...] = s0_ref[...]
        cbuf_ref[SUBLANES - halo:SUBLANES, :] = cp_ref[...]

    cbuf_ref[SUBLANES:SUBLANES + C, :] = x_ref[...]
    conv = None
    for i in range(conv_w):
        term = cbuf_ref[SUBLANES - halo + i:SUBLANES - halo + i + C, :] * cw_ref[i:i + 1, :]
        conv = term if conv is None else conv + term
    cbuf_ref[0:SUBLANES, :] = cbuf_ref[C:C + SUBLANES, :]
    act = conv * _sigmoid(conv)

    sm = sm_ref[...]
    row = c * C + lax.broadcasted_iota(I32, (C, LANES), 0)
    lane = lax.broadcasted_iota(I32, (C, LANES), 1)
    valid = row < t_valid
    la = jnp.where(valid & (lane >= a_off) & (lane < a_off + n_heads), sm, 0.0)
    beta_t = jnp.where(valid, sm, 0.0)
    ri = lax.broadcasted_iota(I32, (C, C), 0)
    ci = lax.broadcasted_iota(I32, (C, C), 1)
    causal = ri >= ci
    strict = ri > ci
    ltri = jnp.where(causal, 1.0, 0.0).astype(BF)
    g_t = sum(jnp.dot(ltri, piece, preferred_element_type=F32) for piece in _split3(la))
    g_rows = jnp.concatenate([g_t, jnp.zeros((LANES - C, LANES), F32)], axis=0).T

    H = range(n_heads)
    dot = functools.partial(jnp.dot, preferred_element_type=F32)
    dot_nt = functools.partial(lax.dot_general, dimension_numbers=NT_DIMS, preferred_element_type=F32)
    qh = [act[:, h * hd:(h + 1) * hd] for h in H]
    kh = [act[:, cd + h * hd:cd + (h + 1) * hd] for h in H]
    vh = [act[:, 2 * cd + h * hd:2 * cd + (h + 1) * hd] for h in H]
    qn = [qh[h] * lax.rsqrt(jnp.sum(qh[h] * qh[h], axis=-1, keepdims=True) + NORM_EPS) * (hd ** -0.5) for h in H]
    kn = [kh[h] * lax.rsqrt(jnp.sum(kh[h] * kh[h], axis=-1, keepdims=True) + NORM_EPS) for h in H]
    beta = [beta_t[:, b_off + h:b_off + h + 1] for h in H]
    g_c = [g_t[:, a_off + h:a_off + h + 1] for h in H]
    g_r = [g_rows[a_off + h:a_off + h + 1, :C] for h in H]
    g_last = [g_t[C - 1:C, a_off + h:a_off + h + 1] for h in H]
    decay = [jnp.where(causal, jnp.exp(jnp.where(causal, g_c[h] - g_r[h], 0.0)), 0.0) for h in H]
    eg = [jnp.exp(g_c[h]) for h in H]
    kb = [kn[h].astype(BF) for h in H]
    qb = [qn[h].astype(BF) for h in H]
    Sb = [S_ref[h].astype(BF) for h in H]
    kk = [dot_nt(kb[h], kb[h]) for h in H]
    qk = [dot_nt(qb[h], kb[h]) * decay[h] for h in H]
    o_state = [dot((qn[h] * eg[h]).astype(BF), Sb[h]) for h in H]
    N = [jnp.where(strict, -(beta[h] * kk[h] * decay[h]), 0.0) for h in H]
    P = list(N)
    for _ in range(int(np.log2(C)) - 1):
        Pb = [P[h].astype(BF) for h in H]
        P = [dot(Pb[h], Pb[h]) for h in H]
        NP = [dot(N[h].astype(BF), P[h].astype(BF)) for h in H]
        N = [N[h] + P[h] + NP[h] for h in H]
    rhs = [jnp.concatenate([vh[h] * beta[h], kn[h] * (beta[h] * eg[h])], axis=1) for h in H]
    corr = [dot(N[h].astype(BF), rhs[h].astype(BF)) for h in H]
    sol = [rhs[h] + corr[h] for h in H]
    su = [dot(sol[h][:, hd:].astype(BF), Sb[h]) for h in H]
    ub = [(sol[h][:, :hd] - su[h]).astype(BF) for h in H]
    o = [o_state[h] + dot(qk[h].astype(BF), ub[h]) for h in H]
    kd = [(kn[h] * jnp.exp(g_last[h] - g_c[h])).astype(BF) for h in H]
    upd = [lax.dot_general(kd[h], ub[h], TN_DIMS, preferred_element_type=F32) for h in H]
    for h in H:
        cols = slice(h * hd, (h + 1) * hd)
        S_ref[h] = S_ref[h] * jnp.exp(g_last[h]) + upd[h]
        gh = gb_ref[:, cols]
        on = o[h] * lax.rsqrt(jnp.mean(o[h] * o[h], axis=-1, keepdims=True) + NORM_EPS) * ng_ref[...]
        ob_ref[:, cols] = (on * (gh * _sigmoid(gh))).astype(BF)

    @pl.when(c == nchunks - 1)
    def _():
        sn_ref[...] = S_ref[...]


def gdn(proj, sm, conv_prev, s0, cw, ng, *, n_batch, rows_per_batch, t_valid, n_heads, hd, col0, b_off, a_off):
    cd = n_heads * hd
    nchunks = rows_per_batch // DN_C
    conv_w = cw.shape[0]
    qkv_blk = col0 // (3 * cd)
    g_blk = (col0 + 3 * cd) // cd
    return pl.pallas_call(
        functools.partial(_gdn_kernel, n_heads=n_heads, hd=hd, t_valid=t_valid, conv_w=conv_w,
                          b_off=b_off, a_off=a_off, nchunks=nchunks),
        grid=(n_batch, nchunks),
        in_specs=[
            pl.BlockSpec((DN_C, 3 * cd), lambda b, c: (b * nchunks + c, qkv_blk)),
            pl.BlockSpec((DN_C, cd), lambda b, c: (b * nchunks + c, g_blk)),
            pl.BlockSpec((DN_C, LANES), lambda b, c: (b * nchunks + c, 0)),
            pl.BlockSpec((None, conv_w - 1, 3 * cd), lambda b, c: (b, 0, 0)),
            pl.BlockSpec((None, n_heads, hd, hd), lambda b, c: (b, 0, 0, 0)),
            pl.BlockSpec((conv_w, 3 * cd), lambda b, c: (0, 0)),
            pl.BlockSpec((1, hd), lambda b, c: (0, 0)),
        ],
        out_specs=[pl.BlockSpec((DN_C, cd), lambda b, c: (b * nchunks + c, 0)),
                   pl.BlockSpec((None, n_heads, hd, hd), lambda b, c: (b, 0, 0, 0))],
        out_shape=[jax.ShapeDtypeStruct((n_batch * rows_per_batch, cd), BF),
                   jax.ShapeDtypeStruct((n_batch, n_heads, hd, hd), F32)],
        scratch_shapes=[pltpu.VMEM((n_heads, hd, hd), F32),
                        pltpu.VMEM((DN_C + SUBLANES, 3 * cd), F32)],
        compiler_params=_cparams(("parallel", "arbitrary")),
        name="gdn",
    )(proj, proj, sm, conv_prev, s0, cw, ng)


def _merge_kernel(h_ref, hb_ref, oa_ref, ob_ref, wga_ref, wgb_ref, woa_ref, wob_ref, wout_ref, g_ref, b_ref,
                  o_ref, acc_ref, *, alpha, nn, rs):
    j = pl.program_id(1)

    @pl.when(j == 0)
    def _():
        acc_ref[...] = jnp.zeros_like(acc_ref)

    hb = hb_ref[...]
    ga = _sigmoid(jnp.dot(hb, wga_ref[...], preferred_element_type=F32))
    gb = _sigmoid(jnp.dot(hb, wgb_ref[...], preferred_element_type=F32))
    y = (ga * jnp.dot(oa_ref[...], woa_ref[...], preferred_element_type=F32)
         + gb * jnp.dot(ob_ref[...], wob_ref[...], preferred_element_type=F32))
    acc_ref[...] += jnp.dot(y.astype(BF), wout_ref[...], preferred_element_type=F32)

    @pl.when(j == nn - 1)
    def _():
        def slab(r, carry):
            rows = pl.ds(pl.multiple_of(r * rs, rs), rs)
            o_ref[rows, :] = _layer_norm(alpha * h_ref[rows, :] + acc_ref[rows, :], g_ref[...], b_ref[...])
            return carry
        lax.fori_loop(0, h_ref.shape[0] // rs, slab, 0)


def merge_ln(h, hb, oa, ob, wgate, woa, wob, wout, g, b, l, alpha, tm, tn):
    M, D = h.shape
    ca, cb = oa.shape[1], ob.shape[1]
    nn = D // tn
    rs = 64 if tm % 64 == 0 else 16
    return pl.pallas_call(
        functools.partial(_merge_kernel, alpha=alpha, nn=nn, rs=rs),
        grid=(M // tm, nn),
        in_specs=[
            pl.BlockSpec((tm, D), lambda i, j: (i, 0)),
            pl.BlockSpec((tm, D), lambda i, j: (i, 0)),
            pl.BlockSpec((tm, ca), lambda i, j: (i, 0)),
            pl.BlockSpec((tm, cb), lambda i, j: (i, 0)),
            pl.BlockSpec((None, D, tn), lambda i, j: (l, 0, j)),
            pl.BlockSpec((None, D, tn), lambda i, j: (l, 0, nn + j)),
            pl.BlockSpec((None, ca, tn), lambda i, j: (l, 0, j)),
            pl.BlockSpec((None, cb, tn), lambda i, j: (l, 0, j)),
            pl.BlockSpec((None, tn, D), lambda i, j: (l, j, 0)),
            pl.BlockSpec((1, D), lambda i, j: (0, 0)),
            pl.BlockSpec((1, D), lambda i, j: (0, 0)),
        ],
        out_specs=pl.BlockSpec((tm, D), lambda i, j: (i, 0)),
        out_shape=jax.ShapeDtypeStruct((M, D), F32),
        scratch_shapes=[pltpu.VMEM((tm, D), F32)],
        compiler_params=_cparams(("parallel", "arbitrary")),
        name="merge_ln",
    )(h, hb, oa, ob, wgate, wgate, woa, wob, wout, g, b)


def _round_up(n, m):
    return -(-n // m) * m


def kernel(x_prompt, x_sample, cache_k, cache_v, cache_kidx, state_conv, state_dn, meta, w_ff_gu, w_ff_down, ln_g, ln_b, w_in, conv_w, idx_ln_g, idx_ln_b, a_log, dt_bias, dn_norm_g, w_oa, w_ob, w_out):
    depth = w_in.shape[0]
    B, S, D = x_prompt.shape
    Bs, Ts, _ = x_sample.shape
    n_meta = meta.shape[0]
    past, HA, HDA = cache_k.shape[2], cache_k.shape[3], cache_k.shape[4]
    CA = HA * HDA
    d_idx = cache_kidx.shape[-1]
    HB, HDB = state_dn.shape[2], state_dn.shape[3]
    CD = HB * HDB
    cw_len = conv_w.shape[1]
    F = w_ff_down.shape[2]
    n_in = w_in.shape[2]
    h_idx = (n_in - 3 * CA - d_idx - 4 * CD - 2 * HB - 2 * D) // (d_idx + 1)
    assert B == 1 and Ts == DN_C and CHUNK == 64 and n_meta <= CHUNK
    assert d_idx + h_idx + 2 * HB <= LANES and h_idx <= SUBLANES and HA * HDA == CA
    assert past % KC == 0 and CA == CD
    alpha = float((2 * depth) ** 0.25)
    T = n_meta + S
    Mp = _round_up(T, ROW_PAD)
    n_sel_p = min(MAX_SEL, S // 4)
    n_sel_s = min(MAX_SEL, (past + Ts) // 4)
    w_off, b_off, a_off = d_idx, d_idx + h_idx, d_idx + h_idx + HB

    Fp = _round_up(F, FF_ALIGN)
    padf = ((0, 0), (0, 0), (0, 0), (0, Fp - F))
    wg = jnp.pad(w_ff_gu[..., :F].astype(BF), padf)
    wu = jnp.pad(w_ff_gu[..., F:].astype(BF), padf)
    wd = jnp.pad(w_ff_down.astype(BF), ((0, 0), (0, 0), (0, Fp - F), (0, 0)))
    offs = np.cumsum([0, CA, CA, CA, h_idx * d_idx, d_idx, h_idx, 3 * CD, CD, HB, HB, D, D])
    span = lambda i, j: w_in[:, :, offs[i]:offs[j]].astype(BF)
    w_main = jnp.concatenate([span(0, 3), span(6, 8)], axis=2)
    pad_small = LANES - (d_idx + h_idx + 2 * HB)
    w_idx = jnp.concatenate([span(3, 6), span(8, 10), jnp.zeros((depth, D, pad_small), BF)], axis=2)
    w_gate = span(10, 12)
    woa, wob, wout = w_oa.astype(BF), w_ob.astype(BF), w_out.astype(BF)

    def lane_vec(v, off):
        return jnp.zeros((depth, 1, LANES), F32).at[:, 0, off:off + v.shape[1]].set(v)
    lng_v, lnb_v = lane_vec(idx_ln_g, 0), lane_vec(idx_ln_b, 0)
    alog_v, dtb_v = lane_vec(a_log, a_off), lane_vec(dt_bias, a_off)

    xp = jnp.concatenate([meta.astype(F32), x_prompt[0], jnp.zeros((Mp - T, D), F32)], axis=0)
    xs = x_sample.reshape(Bs * Ts, D)
    Ms = Bs * Ts
    tm_p, tm_s = _pick_tile(Mp, TM_FFN), _pick_tile(Ms, 512)
    tmm_p, tmm_s = _pick_tile(Mp, TM_MM), _pick_tile(Ms, 1024)
    tf = FF_ALIGN
    tn_mm = _pick_tile(w_main.shape[2], TN_MM)
    tn_mg = _pick_tile(D, TN_MERGE)
    tq = _pick_tile(Mp, TQ)
    tk_p = _pick_tile(Mp, TK_PROMPT)
    tk_c = _pick_tile(past, TK_CACHE)
    assert tq % LANES == 0 and tk_p % KC == 0 and tk_c % KC == 0 and Mp % KC == 0

    kcache, vcache = cache_k, cache_v
    kicache = cache_kidx.astype(BF)
    conv0 = jnp.zeros((1, cw_len - 1, 3 * CD), F32)
    s0 = jnp.zeros((1, HB, HDB, HDB), F32)

    outs = [[] for _ in range(10)]
    for l in range(depth):
        g3 = [ln_g[l, i][None] for i in range(3)]
        b3 = [ln_b[l, i][None] for i in range(3)]
        ng = dn_norm_g[l][None]
        streams = []
        for (x, is_prompt) in ((xp, True), (xs, False)):
            tm, tmm = (tm_p, tmm_p) if is_prompt else (tm_s, tmm_s)
            h, hb = ffn_ln(x, wg, wu, wd, g3[0], b3[0], l, 0, alpha, tm, tf)
            proj = matmul(hb, w_main, l, tmm, tn_mm)
            qi, ki, sm = idx_proj(hb, w_idx, lng_v[l], lnb_v[l], alog_v[l], dtb_v[l], l, tmm, d_idx, h_idx, HB)
            if is_prompt:
                kb16 = proj[:, CA:2 * CA].astype(BF)
                vt4 = proj[:, 2 * CA:3 * CA].astype(BF).reshape(Mp // KC, KC, HA, HDA).transpose(0, 2, 3, 1)
                vt3 = jnp.concatenate([vt4, jnp.ones((Mp // KC, HA, ONES_ROWS, KC), BF)], axis=2)
                vt3 = vt3.reshape(Mp // KC, HA * (HDA + ONES_ROWS), KC)
                oa = dsa_prompt(proj, kb16, vt3, qi, ki, sm.T, t_valid=T, n_meta=n_meta, n_heads=HA, hd=HDA,
                                h_idx=h_idx, d_idx=d_idx, n_sel=n_sel_p, tq=tq, tk=tk_p)
                ob, s_new = gdn(proj, sm, conv0, s0, conv_w[l], ng, n_batch=1, rows_per_batch=Mp, t_valid=T,
                                n_heads=HB, hd=HDB, col0=3 * CA, b_off=b_off, a_off=a_off)
                nrow = T
            else:
                smt = sm.reshape(Bs, Ts, LANES).swapaxes(1, 2)
                oa = dsa_sample(proj, kcache, vcache, qi, ki, kicache, smt, l, n_batch=Bs, t_new=Ts,
                                n_heads=HA, hd=HDA, h_idx=h_idx, d_idx=d_idx, n_sel=n_sel_s, tk=tk_c)
                ob, s_new = gdn(proj, sm, state_conv[l], state_dn[l], conv_w[l], ng, n_batch=Bs, rows_per_batch=Ts,
                                t_valid=Ts, n_heads=HB, hd=HDB, col0=3 * CA, b_off=b_off, a_off=a_off)
                nrow = Ms
            x = merge_ln(h, hb, oa, ob, w_gate, woa, wob, wout, g3[1], b3[1], l, alpha, tm, tn_mg)
            x, _ = ffn_ln(x, wg, wu, wd, g3[2], b3[2], l, 1, alpha, tm, tf)
            nb = 1 if is_prompt else Bs
            per = nrow // nb
            k_new = proj[:nrow, CA:2 * CA].reshape(nb, per, HA, HDA)
            v_new = proj[:nrow, 2 * CA:3 * CA].reshape(nb, per, HA, HDA)
            ki_new = sm[:nrow, :d_idx].reshape(nb, per, d_idx)
            qkv_b = proj[:nrow, 3 * CA:3 * CA + 3 * CD].reshape(nb, per, 3 * CD)
            if is_prompt:
                conv_new = qkv_b[:, per - (cw_len - 1):]
            else:
                conv_new = jnp.concatenate([state_conv[l], qkv_b], axis=1)[:, -(cw_len - 1):]
            streams.append((x, k_new, v_new, ki_new, conv_new, s_new))
        (xp, *st_p), (xs, *st_s) = streams
        for i, a in enumerate(st_p):
            outs[i].append(a)
        for i, a in enumerate(st_s):
            outs[5 + i].append(a)
    y_prompt = xp[n_meta:T][None]
    y_sample = xs.reshape(Bs, Ts, D)
    return (y_prompt, y_sample) + tuple(jnp.stack(o) for o in outs)
```

```python
import functools

import numpy as np
import jax
import jax.numpy as jnp
from jax import lax
from jax.experimental import pallas as pl
from jax.experimental.pallas import tpu as pltpu

CHUNK = 64
MAX_SEL = 256
DN_C = 64
LN_EPS = 1e-5
NORM_EPS = 1e-6

LANES = 128
SUBLANES = 8
KC = 256
FF_ALIGN = 512
VMEM_LIMIT = 56 << 20

ROW_PAD = 1280
TM_FFN = 640
TM_MM = 1280
TN_MM = 1024
TN_MERGE = 512
TQ = 256
TK_PROMPT = 1280
TK_CACHE = 1024

BF = jnp.bfloat16
F32 = jnp.float32
I32 = jnp.int32
I16 = jnp.int16
I16_OFF = 2 ** 15
PACK16 = 16
INT_MIN = -(2 ** 31)
NEG = -3.0e38
FAR = 2.0 ** 100
LOG2E = 1.4426950408889634
ONES_ROWS = 16
NT_DIMS = (((1,), (1,)), ((), ()))
TN_DIMS = (((0,), (0,)), ((), ()))


def _cparams(sem):
    return pltpu.CompilerParams(dimension_semantics=sem, vmem_limit_bytes=VMEM_LIMIT)


def _sigmoid(x):
    return 1.0 / (1.0 + jnp.exp(-x))


def _layer_norm(y, g, b):
    mu = jnp.mean(y, axis=-1, keepdims=True)
    d = y - mu
    var = jnp.mean(d * d, axis=-1, keepdims=True)
    return d * lax.rsqrt(var + LN_EPS) * g + b


def _pick_tile(n, pref):
    if n <= pref:
        return n
    for t in range(pref, 15, -16):
        if n % t == 0:
            return t
    return n


def _ffn_kernel(x_ref, wg_ref, wu_ref, wd_ref, g_ref, b_ref, o_ref, ob_ref, xb_ref, acc_ref, *, alpha, nf, rs):
    j = pl.program_id(1)

    @pl.when(j == 0)
    def _():
        xb_ref[...] = x_ref[...].astype(BF)
        acc_ref[...] = jnp.zeros_like(acc_ref)

    xb = xb_ref[...]
    hg = jnp.dot(xb, wg_ref[...], preferred_element_type=F32)
    hu = jnp.dot(xb, wu_ref[...], preferred_element_type=F32)
    a = (hg * _sigmoid(hg) * hu).astype(BF)
    acc_ref[...] += jnp.dot(a, wd_ref[...], preferred_element_type=F32)

    @pl.when(j == nf - 1)
    def _():
        def slab(r, carry):
            rows = pl.ds(pl.multiple_of(r * rs, rs), rs)
            y = alpha * x_ref[rows, :] + 0.5 * acc_ref[rows, :]
            z = _layer_norm(y, g_ref[...], b_ref[...])
            o_ref[rows, :] = z
            ob_ref[rows, :] = z.astype(BF)
            return carry
        lax.fori_loop(0, x_ref.shape[0] // rs, slab, 0)


def ffn_ln(x, wg, wu, wd, g, b, l, s, alpha, tm, tf):
    M, D = x.shape
    nf = wd.shape[2] // tf
    rs = 64 if tm % 64 == 0 else 16
    return pl.pallas_call(
        functools.partial(_ffn_kernel, alpha=alpha, nf=nf, rs=rs),
        grid=(M // tm, nf),
        in_specs=[
            pl.BlockSpec((tm, D), lambda i, j: (i, 0)),
            pl.BlockSpec((None, None, D, tf), lambda i, j: (l, s, 0, j)),
            pl.BlockSpec((None, None, D, tf), lambda i, j: (l, s, 0, j)),
            pl.BlockSpec((None, None, tf, D), lambda i, j: (l, s, j, 0)),
            pl.BlockSpec((1, D), lambda i, j: (0, 0)),
            pl.BlockSpec((1, D), lambda i, j: (0, 0)),
        ],
        out_specs=[pl.BlockSpec((tm, D), lambda i, j: (i, 0)), pl.BlockSpec((tm, D), lambda i, j: (i, 0))],
        out_shape=[jax.ShapeDtypeStruct((M, D), F32), jax.ShapeDtypeStruct((M, D), BF)],
        scratch_shapes=[pltpu.VMEM((tm, D), BF), pltpu.VMEM((tm, D), F32)],
        compiler_params=_cparams(("parallel", "arbitrary")),
        name="ffn_ln",
    )(x, wg, wu, wd, g, b)


def _mm_kernel(x_ref, w_ref, o_ref):
    o_ref[...] = jnp.dot(x_ref[...], w_ref[...], preferred_element_type=F32)


def matmul(xb, w, l, tm, tn):
    M, D = xb.shape
    N = w.shape[2]
    return pl.pallas_call(
        _mm_kernel,
        grid=(M // tm, N // tn),
        in_specs=[pl.BlockSpec((tm, D), lambda i, j: (i, 0)),
                  pl.BlockSpec((None, D, tn), lambda i, j: (l, 0, j))],
        out_specs=pl.BlockSpec((tm, tn), lambda i, j: (i, j)),
        out_shape=jax.ShapeDtypeStruct((M, N), F32),
        compiler_params=_cparams(("parallel", "arbitrary")),
        name="in_proj",
    )(xb, w)


def _idx_kernel(x_ref, w_ref, lng_ref, lnb_ref, alog_ref, dtb_ref, qi_ref, ki_ref, sm_ref, *, n_qi, d_idx, h_idx, h_b):
    r = jnp.dot(x_ref[...], w_ref[...], preferred_element_type=F32)
    qi_ref[...] = r[:, :n_qi].astype(BF)
    t = r[:, n_qi:]
    lane = lax.broadcasted_iota(I32, t.shape, 1)
    isk = lane < d_idx
    mu = jnp.sum(jnp.where(isk, t, 0.0), axis=-1, keepdims=True) / d_idx
    d = jnp.where(isk, t - mu, 0.0)
    var = jnp.sum(d * d, axis=-1, keepdims=True) / d_idx
    kn = d * lax.rsqrt(var + LN_EPS) * lng_ref[...] + lnb_ref[...]
    wv = t * (h_idx ** -0.5)
    beta = _sigmoid(t)
    z = t + dtb_ref[...]
    softplus = jnp.maximum(z, 0.0) + jnp.log(1.0 + jnp.exp(-jnp.abs(z)))
    la = -jnp.exp(alog_ref[...]) * softplus
    w_off, b_off, a_off = d_idx, d_idx + h_idx, d_idx + h_idx + h_b
    out = jnp.where(isk, kn,
                    jnp.where(lane < b_off, wv,
                              jnp.where(lane < a_off, beta,
                                        jnp.where(lane < a_off + h_b, la, 0.0))))
    sm_ref[...] = out
    ki_ref[...] = kn[:, :d_idx].astype(BF)


def idx_proj(xb, w, lng, lnb, alog, dtb, l, tm, d_idx, h_idx, h_b):
    M, D = xb.shape
    N = w.shape[2]
    n_qi = N - LANES
    vec = pl.BlockSpec((1, LANES), lambda i: (0, 0))
    return pl.pallas_call(
        functools.partial(_idx_kernel, n_qi=n_qi, d_idx=d_idx, h_idx=h_idx, h_b=h_b),
        grid=(M // tm,),
        in_specs=[pl.BlockSpec((tm, D), lambda i: (i, 0)),
                  pl.BlockSpec((None, D, N), lambda i: (l, 0, 0)),
                  vec, vec, vec, vec],
        out_specs=[pl.BlockSpec((tm, n_qi), lambda i: (i, 0)),
                   pl.BlockSpec((tm, d_idx), lambda i: (i, 0)),
                   pl.BlockSpec((tm, LANES), lambda i: (i, 0))],
        out_shape=[jax.ShapeDtypeStruct((M, n_qi), BF),
                   jax.ShapeDtypeStruct((M, d_idx), BF),
                   jax.ShapeDtypeStruct((M, LANES), F32)],
        compiler_params=_cparams(("parallel",)),
        name="idx_proj",
    )(xb, w, lng, lnb, alog, dtb)


def _sort_key(s):
    b = pltpu.bitcast(s + 0.0, I32)
    return b ^ ((b >> 31) & 0x7FFFFFFF)


def _transpose_to(dst_ref, x):
    tq = x.shape[0]
    if tq % LANES:
        x = jnp.concatenate([x, jnp.zeros((LANES - tq % LANES, x.shape[1]), F32)], axis=0)
    dst_ref[...] = x.T[:, :tq].astype(dst_ref.dtype)


def _index_scores(ki_chunk, qit_ref, w_rows, h_idx, d_idx):
    s = [jnp.dot(ki_chunk, qit_ref[h * d_idx:(h + 1) * d_idx, :], preferred_element_type=F32)
         for h in range(h_idx)]
    acc = w_rows[0:1, :] * jnp.maximum(s[0], 0.0)
    for h in range(1, h_idx):
        acc = acc + w_rows[h:h + 1, :] * jnp.maximum(s[h], 0.0)
    return acc


def _store_key(hi_ref, lo_ref, where, key):
    hi_ref[where] = (key >> 16).astype(I16)
    lo_ref[where] = ((key & 0xFFFF) - I16_OFF).astype(I16)


def _count16(nck, tq, pred):
    def body(i, cnt):
        ones = (jnp.where(pred(2 * i), jnp.int16(1), jnp.int16(0))
                + jnp.where(pred(2 * i + 1), jnp.int16(1), jnp.int16(0)))
        parts = [ones[r * PACK16:(r + 1) * PACK16] for r in range(ones.shape[0] // PACK16)]
        while len(parts) > 1:
            parts = [a + b for a, b in zip(parts[::2], parts[1::2])]
        return cnt + parts[0]
    cnt16 = lax.fori_loop(0, (nck + 1) // 2, body, jnp.zeros((PACK16, tq), I16))
    return cnt16.astype(I32).sum(axis=0, keepdims=True)


def _mask_pad_chunk(hi_ref, lo_ref, nck):
    if isinstance(nck, int):
        if nck % 2:
            hi_ref[nck] = jnp.full(hi_ref.shape[1:], -I16_OFF, I16)
            lo_ref[nck] = jnp.full(lo_ref.shape[1:], -I16_OFF, I16)
        return

    @pl.when(nck % 2 == 1)
    def _():
        hi_ref[nck] = jnp.full(hi_ref.shape[1:], -I16_OFF, I16)
        lo_ref[nck] = jnp.full(lo_ref.shape[1:], -I16_OFF, I16)


def _search16(nck, tq, n_sel, base, cnt0, count_ge):
    def bit_body(i, carry):
        thr, cnt_thr = carry
        cand = thr + jnp.left_shift(jnp.int32(1), 15 - i)
        cnt = base + count_ge(cand.astype(I16))
        ok = cnt >= n_sel
        return jnp.where(ok, cand, thr), jnp.where(ok, cnt, cnt_thr)
    return lax.fori_loop(0, 16, bit_body, (jnp.full((1, tq), -I16_OFF, I32), cnt0))


def _select_top(hi_ref, lo_ref, nck, n_sel, tq, idx_bits, cut_ref):
    kc = hi_ref.shape[1]
    zero = jnp.zeros((1, tq), I32)
    _mask_pad_chunk(hi_ref, lo_ref, nck)
    thr_hi, cnt_ge_hi = _search16(nck, tq, n_sel, zero, zero,
                                  lambda t: _count16(nck, tq, lambda c: hi_ref[c] >= t))
    t_hi = thr_hi.astype(I16)
    cnt_gt_hi = _count16(nck, tq, lambda c: hi_ref[c] > t_hi)

    def lo_body(c, carry):
        lo_ref[c] = jnp.where(hi_ref[c] == t_hi, lo_ref[c], jnp.int16(-I16_OFF))
        return carry
    lax.fori_loop(0, nck, lo_body, 0)
    thr_lo, cnt_thr = _search16(nck, tq, n_sel, cnt_gt_hi, cnt_ge_hi,
                                lambda t: _count16(nck, tq, lambda c: lo_ref[c] >= t))
    t_lo = thr_lo.astype(I16)
    has_tie = (thr_hi > -I16_OFF) & (cnt_thr > n_sel)
    cut_ref[...] = jnp.full((1, tq), I16_OFF - 1, I32)

    def kidx16(c):
        return (c * kc + lax.broadcasted_iota(I32, (kc, tq), 0)).astype(I16)

    @pl.when(jnp.max(has_tie.astype(I32)) > 0)
    def _():
        need = n_sel - cnt_gt_hi - _count16(nck, tq, lambda c: (hi_ref[c] == t_hi) & (lo_ref[c] > t_lo))

        def idx_body(i, cut):
            cand = cut + jnp.left_shift(jnp.int32(1), idx_bits - 1 - i)
            c16 = cand.astype(I16)
            cnt = _count16(nck, tq, lambda c: (hi_ref[c] == t_hi) & (lo_ref[c] == t_lo) & (kidx16(c) < c16))
            return jnp.where(cnt < need, cand, cut)

        cut = lax.fori_loop(0, idx_bits, idx_body, zero)
        cut_ref[...] = jnp.where(has_tie, cut, I16_OFF - 1)

    cut = cut_ref[...].astype(I16)

    def bias_body(c, carry):
        hi, lo = hi_ref[c], lo_ref[c]
        sel = (hi > t_hi) | ((hi == t_hi) & ((lo > t_lo) | ((lo == t_lo) & (kidx16(c) <= cut))))
        sel = sel & (hi > jnp.int16(-I16_OFF))
        hi_ref[c] = pltpu.bitcast(jnp.where(sel, jnp.zeros((), BF), jnp.full((), FAR, BF)), I16)
        return carry
    lax.fori_loop(0, nck, bias_body, 0)


def _flash_scores(slot, far16, dist, k_heads, qt_ref, m_ref, lt_ref, p_ref, a_ref, n_heads, hd):
    kc = dist.shape[0]
    dist = dist + pltpu.bitcast(far16, BF).astype(F32)
    for h in range(n_heads):
        cols = slice(h * hd, (h + 1) * hd)
        lt_ref[h, 0:kc, :] = jnp.dot(k_heads[h](), qt_ref[cols, :], preferred_element_type=F32)
    for h in range(n_heads):
        slope = float(2.0 ** (-(8.0 / n_heads) * (h + 1)) * LOG2E)
        lt = lt_ref[h, 0:kc, :] - slope * dist
        m_old = m_ref[h]
        m_new = jnp.maximum(m_old, jnp.max(lt, axis=0, keepdims=True))
        p = jnp.exp2(lt - m_new)
        a = jnp.exp2(m_old - m_new)
        p_ref[slot, h, 0:kc, :] = p.astype(BF)
        a_ref[slot, h] = a
        m_ref[h] = m_new


def _flash_values(slot, kc, vt_heads, l_ref, acc_ref, p_ref, a_ref, n_heads, hd):
    for h in range(n_heads):
        pv = jnp.dot(vt_heads[h](), p_ref[slot, h, 0:kc, :], preferred_element_type=F32)
        a = a_ref[slot, h]
        acc_ref[h] = a * acc_ref[h] + pv[:hd]
        l_ref[h] = a * l_ref[h] + pv[hd:hd + 1]


def _flash_init(m_ref, l_ref, acc_ref):
    m_ref[...] = jnp.full(m_ref.shape, NEG, F32)
    l_ref[...] = jnp.zeros(l_ref.shape, F32)
    acc_ref[...] = jnp.zeros(acc_ref.shape, F32)


def _flash_finish(o_ref, l_ref, acc_ref, n_heads, hd, tq):
    for h in range(n_heads):
        ot = acc_ref[h] / l_ref[h]
        if tq % LANES:
            ot = jnp.concatenate([ot, jnp.zeros((hd, LANES - tq % LANES), F32)], axis=1)
        o_ref[:, h * hd:(h + 1) * hd] = ot.T[:tq].astype(BF)


def _dsa_prompt_kernel(qmap, kmap, nckq, q_ref, k_ref, vt_ref, qi_ref, ki_ref, smt_ref, o_ref,
                       hi_ref, lo_ref, cut_ref, qt_ref, qit_ref, w_ref, m_ref, l_ref, acc_ref, lt_ref, p_ref, a_ref,
                       *, n_heads, hd, h_idx, d_idx, n_meta, t_valid, n_sel, cpb, idx_bits):
    p = pl.program_id(0)
    qb, kb = qmap[p], kmap[p]
    nck = nckq[qb]
    tq = q_ref.shape[0]
    qpos = qb * tq + lax.broadcasted_iota(I32, (KC, tq), 1)

    @pl.when(kb == 0)
    def _():
        _transpose_to(qt_ref, q_ref[...] * (hd ** -0.5 * LOG2E))
        _transpose_to(qit_ref, qi_ref[...].astype(F32))
        w_ref[...] = smt_ref[d_idx:d_idx + SUBLANES, :] * (d_idx ** -0.5)
        qchk = jnp.maximum((qpos - n_meta) >> 6, -1)

        def score_body(c, carry):
            k0 = pl.multiple_of(c * KC, KC)
            sc = _index_scores(ki_ref[pl.ds(k0, KC), :], qit_ref, w_ref[...], h_idx, d_idx)
            kpos = k0 + lax.broadcasted_iota(I32, (KC, tq), 0)
            allowed = (jnp.maximum((kpos - n_meta) >> 6, -1) <= qchk) & (kpos < t_valid)
            _store_key(hi_ref, lo_ref, c, jnp.where(allowed, _sort_key(sc), INT_MIN))
            return carry
        lax.fori_loop(0, nck, score_body, 0)
        _select_top(hi_ref, lo_ref, nck, n_sel, tq, idx_bits, cut_ref)
        _flash_init(m_ref, l_ref, acc_ref)

    vr = vt_ref.shape[1] // n_heads

    def chunk_body(c, carry):
        cg = kb * cpb + c
        rows = pl.ds(pl.multiple_of(c * KC, KC), KC)
        kpos = cg * KC + lax.broadcasted_iota(I32, (KC, tq), 0)
        dist = jnp.abs(qpos - kpos).astype(F32)
        k_heads = [functools.partial(lambda h: k_ref[rows, h * hd:(h + 1) * hd], h) for h in range(n_heads)]
        vt_heads = [functools.partial(lambda h: vt_ref[c, h * vr:(h + 1) * vr, :], h) for h in range(n_heads)]
        _flash_scores(0, hi_ref[cg], dist, k_heads, qt_ref, m_ref, lt_ref, p_ref, a_ref, n_heads, hd)
        _flash_values(0, KC, vt_heads, l_ref, acc_ref, p_ref, a_ref, n_heads, hd)
        return carry
    lax.fori_loop(0, jnp.minimum(cpb, nck - kb * cpb), chunk_body, 0)

    @pl.when((kb + 1) * cpb >= nck)
    def _():
        _flash_finish(o_ref, l_ref, acc_ref, n_heads, hd, tq)


def dsa_prompt(proj, kb16, vt3, qi, ki, smt, *, t_valid, n_meta, n_heads, hd, h_idx, d_idx, n_sel, tq, tk):
    Mp = proj.shape[0]
    C = n_heads * hd
    vr = vt3.shape[1]
    cpb = tk // KC
    nq = Mp // tq
    qlast = np.minimum(np.arange(nq) * tq + tq - 1, t_valid - 1)
    kend = np.minimum(t_valid, n_meta + CHUNK * (np.maximum((qlast - n_meta) // CHUNK, -1) + 1))
    nckq = -(-kend // KC)
    pairs = [(i, j) for i in range(nq) for j in range(-(-int(nckq[i]) // cpb))]
    qmap = jnp.asarray([a for a, _ in pairs], I32)
    kmap = jnp.asarray([b for _, b in pairs], I32)
    idx_bits = max(1, int(np.ceil(np.log2(Mp))))
    grid_spec = pltpu.PrefetchScalarGridSpec(
        num_scalar_prefetch=3,
        grid=(len(pairs),),
        in_specs=[
            pl.BlockSpec((tq, C), lambda p, qm, km, nk: (qm[p], 0)),
            pl.BlockSpec((tk, C), lambda p, qm, km, nk: (km[p], 0)),
            pl.BlockSpec((cpb, vr, KC), lambda p, qm, km, nk: (km[p], 0, 0)),
            pl.BlockSpec((tq, h_idx * d_idx), lambda p, qm, km, nk: (qm[p], 0)),
            pl.BlockSpec((Mp, d_idx), lambda p, qm, km, nk: (0, 0)),
            pl.BlockSpec((LANES, tq), lambda p, qm, km, nk: (0, qm[p])),
        ],
        out_specs=pl.BlockSpec((tq, C), lambda p, qm, km, nk: (qm[p], 0)),
        scratch_shapes=[
            pltpu.VMEM((Mp // KC + 1, KC, tq), I16), pltpu.VMEM((Mp // KC + 1, KC, tq), I16),
            pltpu.VMEM((1, tq), I32),
            pltpu.VMEM((C, tq), BF), pltpu.VMEM((h_idx * d_idx, tq), BF), pltpu.VMEM((SUBLANES, tq), F32),
            pltpu.VMEM((n_heads, 1, tq), F32), pltpu.VMEM((n_heads, 1, tq), F32),
            pltpu.VMEM((n_heads, hd, tq), F32),
            pltpu.VMEM((n_heads, KC, tq), F32), pltpu.VMEM((2, n_heads, KC, tq), BF),
            pltpu.VMEM((2, n_heads, 1, tq), F32),
        ],
    )
    return pl.pallas_call(
        functools.partial(_dsa_prompt_kernel, n_heads=n_heads, hd=hd, h_idx=h_idx, d_idx=d_idx, n_meta=n_meta,
                          t_valid=t_valid, n_sel=n_sel, cpb=cpb, idx_bits=idx_bits),
        grid_spec=grid_spec,
        out_shape=jax.ShapeDtypeStruct((Mp, C), BF),
        compiler_params=_cparams(("arbitrary",)),
        name="dsa_prompt",
    )(qmap, kmap, jnp.asarray(nckq, I32), proj, kb16, vt3, qi, ki, smt)


def _dsa_sample_kernel(q_ref, kn_ref, vn_ref, kc_ref, vc_ref, qi_ref, kin_ref, kic_ref, smt_ref, o_ref,
                       hi_ref, lo_ref, cut_ref, qt_ref, qit_ref, w_ref, m_ref, l_ref, acc_ref, lt_ref, p_ref, a_ref,
                       *, n_heads, hd, h_idx, d_idx, past, n_sel, cpb, nkb, idx_bits):
    kb = pl.program_id(1)
    tq = q_ref.shape[0]
    ncache = past // KC
    nck = ncache + 1
    qpos = past + lax.broadcasted_iota(I32, (KC, tq), 1)

    @pl.when(kb == 0)
    def _():
        _transpose_to(qt_ref, q_ref[...] * (hd ** -0.5 * LOG2E))
        _transpose_to(qit_ref, qi_ref[...].astype(F32))
        w_ref[...] = smt_ref[d_idx:d_idx + SUBLANES, :] * (d_idx ** -0.5)

        def score_body(c, carry):
            k0 = pl.multiple_of(c * KC, KC)
            sc = _index_scores(kic_ref[pl.ds(k0, KC), :], qit_ref, w_ref[...], h_idx, d_idx)
            _store_key(hi_ref, lo_ref, c, _sort_key(sc))
            return carry
        lax.fori_loop(0, ncache, score_body, 0)
        hi_ref[ncache] = jnp.full((KC, tq), -I16_OFF, I16)
        lo_ref[ncache] = jnp.full((KC, tq), -I16_OFF, I16)
        sc = _index_scores(kin_ref[...], qit_ref, w_ref[...], h_idx, d_idx)
        _store_key(hi_ref, lo_ref, (ncache, slice(0, tq), slice(None)), _sort_key(sc))
        _select_top(hi_ref, lo_ref, nck, n_sel, tq, idx_bits, cut_ref)
        _flash_init(m_ref, l_ref, acc_ref)

    def vt_of(v):
        kc = v.shape[0]
        if kc % LANES:
            v = jnp.concatenate([v, jnp.zeros((LANES - kc % LANES, hd), F32)], axis=0)
        return jnp.concatenate([v.T[:, :kc].astype(BF), jnp.ones((ONES_ROWS, kc), BF)], axis=0)

    vall = [None, None]
    for c in range(cpb):
        cg = kb * cpb + c
        kall = jnp.swapaxes(kc_ref[c * KC:(c + 1) * KC], 0, 1)
        kpos = cg * KC + lax.broadcasted_iota(I32, (KC, tq), 0)
        dist = jnp.abs(qpos - kpos).astype(F32)
        k_heads = [functools.partial(lambda h, kall=kall: kall[h].astype(BF), h) for h in range(n_heads)]
        if c > 0:
            vt_heads = [functools.partial(lambda h, v=vall[(c - 1) % 2]: vt_of(v[h]), h) for h in range(n_heads)]
            _flash_values((c - 1) % 2, KC, vt_heads, l_ref, acc_ref, p_ref, a_ref, n_heads, hd)
        vall[c % 2] = jnp.swapaxes(vc_ref[c * KC:(c + 1) * KC], 0, 1)
        _flash_scores(c % 2, hi_ref[cg], dist, k_heads, qt_ref, m_ref, lt_ref, p_ref, a_ref, n_heads, hd)
    vt_heads = [functools.partial(lambda h, v=vall[(cpb - 1) % 2]: vt_of(v[h]), h) for h in range(n_heads)]
    _flash_values((cpb - 1) % 2, KC, vt_heads, l_ref, acc_ref, p_ref, a_ref, n_heads, hd)

    @pl.when(kb == nkb - 1)
    def _():
        tn = kn_ref.shape[0]
        kpos = past + lax.broadcasted_iota(I32, (tn, tq), 0)
        dist = jnp.abs(past + lax.broadcasted_iota(I32, (tn, tq), 1) - kpos).astype(F32)
        k_heads = [functools.partial(lambda h: kn_ref[:, h * hd:(h + 1) * hd].astype(BF), h) for h in range(n_heads)]
        vt_heads = [functools.partial(lambda h: vt_of(vn_ref[:, h * hd:(h + 1) * hd]), h) for h in range(n_heads)]
        _flash_scores(0, hi_ref[ncache, 0:tn, :], dist, k_heads, qt_ref, m_ref, lt_ref, p_ref, a_ref, n_heads, hd)
        _flash_values(0, tn, vt_heads, l_ref, acc_ref, p_ref, a_ref, n_heads, hd)
        _flash_finish(o_ref, l_ref, acc_ref, n_heads, hd, tq)


def dsa_sample(proj, kcache, vcache, qi, ki, kicache, smt, l, *, n_batch, t_new, n_heads, hd, h_idx, d_idx, n_sel, tk):
    C = n_heads * hd
    past = kcache.shape[2]
    cpb = tk // KC
    nkb = past // tk
    idx_bits = max(1, int(np.ceil(np.log2(past + KC))))
    tq = t_new
    return pl.pallas_call(
        functools.partial(_dsa_sample_kernel, n_heads=n_heads, hd=hd, h_idx=h_idx, d_idx=d_idx, past=past,
                          n_sel=n_sel, cpb=cpb, nkb=nkb, idx_bits=idx_bits),
        grid=(n_batch, nkb),
        in_specs=[
            pl.BlockSpec((tq, C), lambda b, j: (b, 0)),
            pl.BlockSpec((tq, C), lambda b, j: (b, 1)),
            pl.BlockSpec((tq, C), lambda b, j: (b, 2)),
            pl.BlockSpec((None, None, tk, n_heads, hd), lambda b, j: (l, b, j, 0, 0)),
            pl.BlockSpec((None, None, tk, n_heads, hd), lambda b, j: (l, b, j, 0, 0)),
            pl.BlockSpec((tq, h_idx * d_idx), lambda b, j: (b, 0)),
            pl.BlockSpec((tq, d_idx), lambda b, j: (b, 0)),
            pl.BlockSpec((None, None, past, d_idx), lambda b, j: (l, b, 0, 0)),
            pl.BlockSpec((None, LANES, tq), lambda b, j: (b, 0, 0)),
        ],
        out_specs=pl.BlockSpec((tq, C), lambda b, j: (b, 0)),
        out_shape=jax.ShapeDtypeStruct((n_batch * tq, C), BF),
        scratch_shapes=[
            pltpu.VMEM((past // KC + 2, KC, tq), I16), pltpu.VMEM((past // KC + 2, KC, tq), I16),
            pltpu.VMEM((1, tq), I32),
            pltpu.VMEM((C, tq), BF), pltpu.VMEM((h_idx * d_idx, tq), BF), pltpu.VMEM((SUBLANES, tq), F32),
            pltpu.VMEM((n_heads, 1, tq), F32), pltpu.VMEM((n_heads, 1, tq), F32),
            pltpu.VMEM((n_heads, hd, tq), F32),
            pltpu.VMEM((n_heads, KC, tq), F32), pltpu.VMEM((2, n_heads, KC, tq), BF),
            pltpu.VMEM((2, n_heads, 1, tq), F32),
        ],
        compiler_params=_cparams(("parallel", "arbitrary")),
        name="dsa_sample",
    )(proj, proj, proj, kcache, vcache, qi, ki, kicache, smt)


def _split3(x):
    hi = x.astype(BF)
    r = x - hi.astype(F32)
    mid = r.astype(BF)
    lo = (r - mid.astype(F32)).astype(BF)
    return hi, mid, lo


def _gdn_kernel(x_ref, gb_ref, sm_ref, cp_ref, s0_ref, cw_ref, ng_ref, ob_ref, sn_ref, S_ref, cbuf_ref,
                *, n_heads, hd, t_valid, conv_w, b_off, a_off, nchunks):
    c = pl.program_id(1)
    C = x_ref.shape[0]
    cd = n_heads * hd
    halo = conv_w - 1

    @pl.when(c == 0)
    def _():
        S_ref[...] = s0_ref[...]
        cbuf_ref[SUBLANES - halo:SUBLANES, :] = cp_ref[...]

    cbuf_ref[SUBLANES:SUBLANES + C, :] = x_ref[...]
    conv = None
    for i in range(conv_w):
        term = cbuf_ref[SUBLANES - halo + i:SUBLANES - halo + i + C, :] * cw_ref[i:i + 1, :]
        conv = term if conv is None else conv + term
    cbuf_ref[0:SUBLANES, :] = cbuf_ref[C:C + SUBLANES, :]
    act = conv * _sigmoid(conv)

    sm = sm_ref[...]
    row = c * C + lax.broadcasted_iota(I32, (C, LANES), 0)
    lane = lax.broadcasted_iota(I32, (C, LANES), 1)
    valid = row < t_valid
    la = jnp.where(valid & (lane >= a_off) & (lane < a_off + n_heads), sm, 0.0)
    beta_t = jnp.where(valid, sm, 0.0)
    ri = lax.broadcasted_iota(I32, (C, C), 0)
    ci = lax.broadcasted_iota(I32, (C, C), 1)
    causal = ri >= ci
    strict = ri > ci
    ltri = jnp.where(causal, 1.0, 0.0).astype(BF)
    g_t = sum(jnp.dot(ltri, piece, preferred_element_type=F32) for piece in _split3(la))
    g_rows = jnp.concatenate([g_t, jnp.zeros((LANES - C, LANES), F32)], axis=0).T

    H = range(n_heads)
    dot = functools.partial(jnp.dot, preferred_element_type=F32)
    dot_nt = functools.partial(lax.dot_general, dimension_numbers=NT_DIMS, preferred_element_type=F32)
    qh = [act[:, h * hd:(h + 1) * hd] for h in H]
    kh = [act[:, cd + h * hd:cd + (h + 1) * hd] for h in H]
    vh = [act[:, 2 * cd + h * hd:2 * cd + (h + 1) * hd] for h in H]
    qn = [qh[h] * lax.rsqrt(jnp.sum(qh[h] * qh[h], axis=-1, keepdims=True) + NORM_EPS) * (hd ** -0.5) for h in H]
    kn = [kh[h] * lax.rsqrt(jnp.sum(kh[h] * kh[h], axis=-1, keepdims=True) + NORM_EPS) for h in H]
    beta = [beta_t[:, b_off + h:b_off + h + 1] for h in H]
    g_c = [g_t[:, a_off + h:a_off + h + 1] for h in H]
    g_r = [g_rows[a_off + h:a_off + h + 1, :C] for h in H]
    g_last = [g_t[C - 1:C, a_off + h:a_off + h + 1] for h in H]
    decay = [jnp.where(causal, jnp.exp(jnp.where(causal, g_c[h] - g_r[h], 0.0)), 0.0) for h in H]
    eg = [jnp.exp(g_c[h]) for h in H]
    kb = [kn[h].astype(BF) for h in H]
    qb = [qn[h].astype(BF) for h in H]
    Sb = [S_ref[h].astype(BF) for h in H]
    kk = [dot_nt(kb[h], kb[h]) for h in H]
    qk = [dot_nt(qb[h], kb[h]) * decay[h] for h in H]
    o_state = [dot((qn[h] * eg[h]).astype(BF), Sb[h]) for h in H]
    N = [jnp.where(strict, -(beta[h] * kk[h] * decay[h]), 0.0) for h in H]
    P = list(N)
    for _ in range(int(np.log2(C)) - 1):
        Pb = [P[h].astype(BF) for h in H]
        P = [dot(Pb[h], Pb[h]) for h in H]
        NP = [dot(N[h].astype(BF), P[h].astype(BF)) for h in H]
        N = [N[h] + P[h] + NP[h] for h in H]
    rhs = [jnp.concatenate([vh[h] * beta[h], kn[h] * (beta[h] * eg[h])], axis=1) for h in H]
    corr = [dot(N[h].astype(BF), rhs[h].astype(BF)) for h in H]
    sol = [rhs[h] + corr[h] for h in H]
    su = [dot(sol[h][:, hd:].astype(BF), Sb[h]) for h in H]
    ub = [(sol[h][:, :hd] - su[h]).astype(BF) for h in H]
    o = [o_state[h] + dot(qk[h].astype(BF), ub[h]) for h in H]
    kd = [(kn[h] * jnp.exp(g_last[h] - g_c[h])).astype(BF) for h in H]
    upd = [lax.dot_general(kd[h], ub[h], TN_DIMS, preferred_element_type=F32) for h in H]
    for h in H:
        cols = slice(h * hd, (h + 1) * hd)
        S_ref[h] = S_ref[h] * jnp.exp(g_last[h]) + upd[h]
        gh = gb_ref[:, cols]
        on = o[h] * lax.rsqrt(jnp.mean(o[h] * o[h], axis=-1, keepdims=True) + NORM_EPS) * ng_ref[...]
        ob_ref[:, cols] = (on * (gh * _sigmoid(gh))).astype(BF)

    @pl.when(c == nchunks - 1)
    def _():
        sn_ref[...] = S_ref[...]


def gdn(proj, sm, conv_prev, s0, cw, ng, *, n_batch, rows_per_batch, t_valid, n_heads, hd, col0, b_off, a_off):
    cd = n_heads * hd
    nchunks = rows_per_batch // DN_C
    conv_w = cw.shape[0]
    qkv_blk = col0 // (3 * cd)
    g_blk = (col0 + 3 * cd) // cd
    return pl.pallas_call(
        functools.partial(_gdn_kernel, n_heads=n_heads, hd=hd, t_valid=t_valid, conv_w=conv_w,
                          b_off=b_off, a_off=a_off, nchunks=nchunks),
        grid=(n_batch, nchunks),
        in_specs=[
            pl.BlockSpec((DN_C, 3 * cd), lambda b, c: (b * nchunks + c, qkv_blk)),
            pl.BlockSpec((DN_C, cd), lambda b, c: (b * nchunks + c, g_blk)),
            pl.BlockSpec((DN_C, LANES), lambda b, c: (b * nchunks + c, 0)),
            pl.BlockSpec((None, conv_w - 1, 3 * cd), lambda b, c: (b, 0, 0)),
            pl.BlockSpec((None, n_heads, hd, hd), lambda b, c: (b, 0, 0, 0)),
            pl.BlockSpec((conv_w, 3 * cd), lambda b, c: (0, 0)),
            pl.BlockSpec((1, hd), lambda b, c: (0, 0)),
        ],
        out_specs=[pl.BlockSpec((DN_C, cd), lambda b, c: (b * nchunks + c, 0)),
                   pl.BlockSpec((None, n_heads, hd, hd), lambda b, c: (b, 0, 0, 0))],
        out_shape=[jax.ShapeDtypeStruct((n_batch * rows_per_batch, cd), BF),
                   jax.ShapeDtypeStruct((n_batch, n_heads, hd, hd), F32)],
        scratch_shapes=[pltpu.VMEM((n_heads, hd, hd), F32),
                        pltpu.VMEM((DN_C + SUBLANES, 3 * cd), F32)],
        compiler_params=_cparams(("parallel", "arbitrary")),
        name="gdn",
    )(proj, proj, sm, conv_prev, s0, cw, ng)


def _merge_kernel(h_ref, hb_ref, oa_ref, ob_ref, wga_ref, wgb_ref, woa_ref, wob_ref, wout_ref, g_ref, b_ref,
                  o_ref, acc_ref, *, alpha, nn, rs):
    j = pl.program_id(1)

    @pl.when(j == 0)
    def _():
        acc_ref[...] = jnp.zeros_like(acc_ref)

    hb = hb_ref[...]
    ga = _sigmoid(jnp.dot(hb, wga_ref[...], preferred_element_type=F32))
    gb = _sigmoid(jnp.dot(hb, wgb_ref[...], preferred_element_type=F32))
    y = (ga * jnp.dot(oa_ref[...], woa_ref[...], preferred_element_type=F32)
         + gb * jnp.dot(ob_ref[...], wob_ref[...], preferred_element_type=F32))
    acc_ref[...] += jnp.dot(y.astype(BF), wout_ref[...], preferred_element_type=F32)

    @pl.when(j == nn - 1)
    def _():
        def slab(r, carry):
            rows = pl.ds(pl.multiple_of(r * rs, rs), rs)
            o_ref[rows, :] = _layer_norm(alpha * h_ref[rows, :] + acc_ref[rows, :], g_ref[...], b_ref[...])
            return carry
        lax.fori_loop(0, h_ref.shape[0] // rs, slab, 0)


def merge_ln(h, hb, oa, ob, wgate, woa, wob, wout, g, b, l, alpha, tm, tn):
    M, D = h.shape
    ca, cb = oa.shape[1], ob.shape[1]
    nn = D // tn
    rs = 64 if tm % 64 == 0 else 16
    return pl.pallas_call(
        functools.partial(_merge_kernel, alpha=alpha, nn=nn, rs=rs),
        grid=(M // tm, nn),
        in_specs=[
            pl.BlockSpec((tm, D), lambda i, j: (i, 0)),
            pl.BlockSpec((tm, D), lambda i, j: (i, 0)),
            pl.BlockSpec((tm, ca), lambda i, j: (i, 0)),
            pl.BlockSpec((tm, cb), lambda i, j: (i, 0)),
            pl.BlockSpec((None, D, tn), lambda i, j: (l, 0, j)),
            pl.BlockSpec((None, D, tn), lambda i, j: (l, 0, nn + j)),
            pl.BlockSpec((None, ca, tn), lambda i, j: (l, 0, j)),
            pl.BlockSpec((None, cb, tn), lambda i, j: (l, 0, j)),
            pl.BlockSpec((None, tn, D), lambda i, j: (l, j, 0)),
            pl.BlockSpec((1, D), lambda i, j: (0, 0)),
            pl.BlockSpec((1, D), lambda i, j: (0, 0)),
        ],
        out_specs=pl.BlockSpec((tm, D), lambda i, j: (i, 0)),
        out_shape=jax.ShapeDtypeStruct((M, D), F32),
        scratch_shapes=[pltpu.VMEM((tm, D), F32)],
        compiler_params=_cparams(("parallel", "arbitrary")),
        name="merge_ln",
    )(h, hb, oa, ob, wgate, wgate, woa, wob, wout, g, b)


def _round_up(n, m):
    return -(-n // m) * m


def kernel(x_prompt, x_sample, cache_k, cache_v, cache_kidx, state_conv, state_dn, meta, w_ff_gu, w_ff_down, ln_g, ln_b, w_in, conv_w, idx_ln_g, idx_ln_b, a_log, dt_bias, dn_norm_g, w_oa, w_ob, w_out):
    depth = w_in.shape[0]
    B, S, D = x_prompt.shape
    Bs, Ts, _ = x_sample.shape
    n_meta = meta.shape[0]
    past, HA, HDA = cache_k.shape[2], cache_k.shape[3], cache_k.shape[4]
    CA = HA * HDA
    d_idx = cache_kidx.shape[-1]
    HB, HDB = state_dn.shape[2], state_dn.shape[3]
    CD = HB * HDB
    cw_len = conv_w.shape[1]
    F = w_ff_down.shape[2]
    n_in = w_in.shape[2]
    h_idx = (n_in - 3 * CA - d_idx - 4 * CD - 2 * HB - 2 * D) // (d_idx + 1)
    assert B == 1 and Ts == DN_C and CHUNK == 64 and n_meta <= CHUNK
    assert d_idx + h_idx + 2 * HB <= LANES and h_idx <= SUBLANES and HA * HDA == CA
    assert past % KC == 0 and CA == CD
    alpha = float((2 * depth) ** 0.25)
    T = n_meta + S
    Mp = _round_up(T, ROW_PAD)
    n_sel_p = min(MAX_SEL, S // 4)
    n_sel_s = min(MAX_SEL, (past + Ts) // 4)
    w_off, b_off, a_off = d_idx, d_idx + h_idx, d_idx + h_idx + HB

    Fp = _round_up(F, FF_ALIGN)
    padf = ((0, 0), (0, 0), (0, 0), (0, Fp - F))
    wg = jnp.pad(w_ff_gu[..., :F].astype(BF), padf)
    wu = jnp.pad(w_ff_gu[..., F:].astype(BF), padf)
    wd = jnp.pad(w_ff_down.astype(BF), ((0, 0), (0, 0), (0, Fp - F), (0, 0)))
    offs = np.cumsum([0, CA, CA, CA, h_idx * d_idx, d_idx, h_idx, 3 * CD, CD, HB, HB, D, D])
    span = lambda i, j: w_in[:, :, offs[i]:offs[j]].astype(BF)
    w_main = jnp.concatenate([span(0, 3), span(6, 8)], axis=2)
    pad_small = LANES - (d_idx + h_idx + 2 * HB)
    w_idx = jnp.concatenate([span(3, 6), span(8, 10), jnp.zeros((depth, D, pad_small), BF)], axis=2)
    w_gate = span(10, 12)
    woa, wob, wout = w_oa.astype(BF), w_ob.astype(BF), w_out.astype(BF)

    def lane_vec(v, off):
        return jnp.zeros((depth, 1, LANES), F32).at[:, 0, off:off + v.shape[1]].set(v)
    lng_v, lnb_v = lane_vec(idx_ln_g, 0), lane_vec(idx_ln_b, 0)
    alog_v, dtb_v = lane_vec(a_log, a_off), lane_vec(dt_bias, a_off)

    xp = jnp.concatenate([meta.astype(F32), x_prompt[0], jnp.zeros((Mp - T, D), F32)], axis=0)
    xs = x_sample.reshape(Bs * Ts, D)
    Ms = Bs * Ts
    tm_p, tm_s = _pick_tile(Mp, TM_FFN), _pick_tile(Ms, 512)
    tmm_p, tmm_s = _pick_tile(Mp, TM_MM), _pick_tile(Ms, 1024)
    tf = FF_ALIGN
    tn_mm = _pick_tile(w_main.shape[2], TN_MM)
    tn_mg = _pick_tile(D, TN_MERGE)
    tq = _pick_tile(Mp, TQ)
    tk_p = _pick_tile(Mp, TK_PROMPT)
    tk_c = _pick_tile(past, TK_CACHE)
    assert tq % LANES == 0 and tk_p % KC == 0 and tk_c % KC == 0 and Mp % KC == 0

    kcache, vcache = cache_k, cache_v
    kicache = cache_kidx.astype(BF)
    conv0 = jnp.zeros((1, cw_len - 1, 3 * CD), F32)
    s0 = jnp.zeros((1, HB, HDB, HDB), F32)

    outs = [[] for _ in range(10)]
    for l in range(depth):
        g3 = [ln_g[l, i][None] for i in range(3)]
        b3 = [ln_b[l, i][None] for i in range(3)]
        ng = dn_norm_g[l][None]
        streams = []
        for (x, is_prompt) in ((xp, True), (xs, False)):
            tm, tmm = (tm_p, tmm_p) if is_prompt else (tm_s, tmm_s)
            h, hb = ffn_ln(x, wg, wu, wd, g3[0], b3[0], l, 0, alpha, tm, tf)
            proj = matmul(hb, w_main, l, tmm, tn_mm)
            qi, ki, sm = idx_proj(hb, w_idx, lng_v[l], lnb_v[l], alog_v[l], dtb_v[l], l, tmm, d_idx, h_idx, HB)
            if is_prompt:
                kb16 = proj[:, CA:2 * CA].astype(BF)
                vt4 = proj[:, 2 * CA:3 * CA].astype(BF).reshape(Mp // KC, KC, HA, HDA).transpose(0, 2, 3, 1)
                vt3 = jnp.concatenate([vt4, jnp.ones((Mp // KC, HA, ONES_ROWS, KC), BF)], axis=2)
                vt3 = vt3.reshape(Mp // KC, HA * (HDA + ONES_ROWS), KC)
                oa = dsa_prompt(proj, kb16, vt3, qi, ki, sm.T, t_valid=T, n_meta=n_meta, n_heads=HA, hd=HDA,
                                h_idx=h_idx, d_idx=d_idx, n_sel=n_sel_p, tq=tq, tk=tk_p)
                ob, s_new = gdn(proj, sm, conv0, s0, conv_w[l], ng, n_batch=1, rows_per_batch=Mp, t_valid=T,
                                n_heads=HB, hd=HDB, col0=3 * CA, b_off=b_off, a_off=a_off)
                nrow = T
            else:
                smt = sm.reshape(Bs, Ts, LANES).swapaxes(1, 2)
                oa = dsa_sample(proj, kcache, vcache, qi, ki, kicache, smt, l, n_batch=Bs, t_new=Ts,
                                n_heads=HA, hd=HDA, h_idx=h_idx, d_idx=d_idx, n_sel=n_sel_s, tk=tk_c)
                ob, s_new = gdn(proj, sm, state_conv[l], state_dn[l], conv_w[l], ng, n_batch=Bs, rows_per_batch=Ts,
                                t_valid=Ts, n_heads=HB, hd=HDB, col0=3 * CA, b_off=b_off, a_off=a_off)
                nrow = Ms
            x = merge_ln(h, hb, oa, ob, w_gate, woa, wob, wout, g3[1], b3[1], l, alpha, tm, tn_mg)
            x, _ = ffn_ln(x, wg, wu, wd, g3[2], b3[2], l, 1, alpha, tm, tf)
            nb = 1 if is_prompt else Bs
            per = nrow // nb
            qkv_b = proj[:nrow, 3 * CA:3 * CA + 3 * CD].reshape(nb, per, 3 * CD)
            if is_prompt:
                conv_new = qkv_b[:, per - (cw_len - 1):]
            else:
                conv_new = jnp.concatenate([state_conv[l], qkv_b], axis=1)[:, -(cw_len - 1):]
            streams.append((x, proj[:nrow, CA:2 * CA], proj[:nrow, 2 * CA:3 * CA], sm[:nrow, :d_idx], conv_new, s_new))
        (xp, *st_p), (xs, *st_s) = streams
        for i, a in enumerate(st_p):
            outs[i].append(a)
        for i, a in enumerate(st_s):
            outs[5 + i].append(a)
    y_prompt = xp[n_meta:T][None]
    y_sample = xs.reshape(Bs, Ts, D)
    res = []
    for i, o in enumerate(outs):
        a = jnp.stack(o)
        nb, per = (1, T) if i < 5 else (Bs, Ts)
        if i % 5 < 2:
            a = a.reshape(depth, nb, per, HA, HDA)
        elif i % 5 == 2:
            a = a.reshape(depth, nb, per, d_idx)
        res.append(a)
    return (y_prompt, y_sample) + tuple(res)
```

```python
import functools

import numpy as np
import jax
import jax.numpy as jnp
from jax import lax
from jax.experimental import pallas as pl
from jax.experimental.pallas import tpu as pltpu

CHUNK = 64
MAX_SEL = 256
DN_C = 64
LN_EPS = 1e-5
NORM_EPS = 1e-6

LANES = 128
SUBLANES = 8
KC = 256
FF_ALIGN = 512
VMEM_LIMIT = 56 << 20

ROW_PAD = 1280
TM_FFN = 640
TM_MM = 1280
TN_MM = 1024
TN_MERGE = 512
TQ = 256
TK_PROMPT = 1280
TK_CACHE = 1024

BF = jnp.bfloat16
F32 = jnp.float32
I32 = jnp.int32
I16 = jnp.int16
I16_OFF = 2 ** 15
PACK16 = 16
INT_MIN = -(2 ** 31)
NEG = -3.0e38
FAR = 2.0 ** 100
LOG2E = 1.4426950408889634
ONES_ROWS = 16
NT_DIMS = (((1,), (1,)), ((), ()))
TN_DIMS = (((0,), (0,)), ((), ()))


def _cparams(sem):
    return pltpu.CompilerParams(dimension_semantics=sem, vmem_limit_bytes=VMEM_LIMIT)


def _sigmoid(x):
    return 1.0 / (1.0 + jnp.exp(-x))


def _layer_norm(y, g, b):
    mu = jnp.mean(y, axis=-1, keepdims=True)
    d = y - mu
    var = jnp.mean(d * d, axis=-1, keepdims=True)
    return d * lax.rsqrt(var + LN_EPS) * g + b


def _pick_tile(n, pref):
    if n <= pref:
        return n
    for t in range(pref, 15, -16):
        if n % t == 0:
            return t
    return n


def _ffn_kernel(x_ref, wg_ref, wu_ref, wd_ref, g_ref, b_ref, o_ref, ob_ref, xb_ref, acc_ref, *, alpha, nf, rs):
    j = pl.program_id(1)

    @pl.when(j == 0)
    def _():
        xb_ref[...] = x_ref[...].astype(BF)
        acc_ref[...] = jnp.zeros_like(acc_ref)

    xb = xb_ref[...]
    hg = jnp.dot(xb, wg_ref[...], preferred_element_type=F32)
    hu = jnp.dot(xb, wu_ref[...], preferred_element_type=F32)
    a = (hg * _sigmoid(hg) * hu).astype(BF)
    acc_ref[...] += jnp.dot(a, wd_ref[...], preferred_element_type=F32)

    @pl.when(j == nf - 1)
    def _():
        def slab(r, carry):
            rows = pl.ds(pl.multiple_of(r * rs, rs), rs)
            y = alpha * x_ref[rows, :] + 0.5 * acc_ref[rows, :]
            z = _layer_norm(y, g_ref[...], b_ref[...])
            o_ref[rows, :] = z
            ob_ref[rows, :] = z.astype(BF)
            return carry
        lax.fori_loop(0, x_ref.shape[0] // rs, slab, 0)


def ffn_ln(x, wg, wu, wd, g, b, l, s, alpha, tm, tf):
    M, D = x.shape
    nf = wd.shape[2] // tf
    rs = 64 if tm % 64 == 0 else 16
    return pl.pallas_call(
        functools.partial(_ffn_kernel, alpha=alpha, nf=nf, rs=rs),
        grid=(M // tm, nf),
        in_specs=[
            pl.BlockSpec((tm, D), lambda i, j: (i, 0)),
            pl.BlockSpec((None, None, D, tf), lambda i, j: (l, s, 0, j)),
            pl.BlockSpec((None, None, D, tf), lambda i, j: (l, s, 0, j)),
            pl.BlockSpec((None, None, tf, D), lambda i, j: (l, s, j, 0)),
            pl.BlockSpec((1, D), lambda i, j: (0, 0)),
            pl.BlockSpec((1, D), lambda i, j: (0, 0)),
        ],
        out_specs=[pl.BlockSpec((tm, D), lambda i, j: (i, 0)), pl.BlockSpec((tm, D), lambda i, j: (i, 0))],
        out_shape=[jax.ShapeDtypeStruct((M, D), F32), jax.ShapeDtypeStruct((M, D), BF)],
        scratch_shapes=[pltpu.VMEM((tm, D), BF), pltpu.VMEM((tm, D), F32)],
        compiler_params=_cparams(("parallel", "arbitrary")),
        name="ffn_ln",
    )(x, wg, wu, wd, g, b)


def _mm_kernel(x_ref, w_ref, o_ref):
    o_ref[...] = jnp.dot(x_ref[...], w_ref[...], preferred_element_type=F32)


def matmul(xb, w, l, tm, tn):
    M, D = xb.shape
    N = w.shape[2]
    return pl.pallas_call(
        _mm_kernel,
        grid=(M // tm, N // tn),
        in_specs=[pl.BlockSpec((tm, D), lambda i, j: (i, 0)),
                  pl.BlockSpec((None, D, tn), lambda i, j: (l, 0, j))],
        out_specs=pl.BlockSpec((tm, tn), lambda i, j: (i, j)),
        out_shape=jax.ShapeDtypeStruct((M, N), F32),
        compiler_params=_cparams(("parallel", "arbitrary")),
        name="in_proj",
    )(xb, w)


def _idx_kernel(x_ref, w_ref, lng_ref, lnb_ref, alog_ref, dtb_ref, qi_ref, ki_ref, sm_ref, *, n_qi, d_idx, h_idx, h_b):
    r = jnp.dot(x_ref[...], w_ref[...], preferred_element_type=F32)
    qi_ref[...] = r[:, :n_qi].astype(BF)
    t = r[:, n_qi:]
    lane = lax.broadcasted_iota(I32, t.shape, 1)
    isk = lane < d_idx
    mu = jnp.sum(jnp.where(isk, t, 0.0), axis=-1, keepdims=True) / d_idx
    d = jnp.where(isk, t - mu, 0.0)
    var = jnp.sum(d * d, axis=-1, keepdims=True) / d_idx
    kn = d * lax.rsqrt(var + LN_EPS) * lng_ref[...] + lnb_ref[...]
    wv = t * (h_idx ** -0.5)
    beta = _sigmoid(t)
    z = t + dtb_ref[...]
    softplus = jnp.maximum(z, 0.0) + jnp.log(1.0 + jnp.exp(-jnp.abs(z)))
    la = -jnp.exp(alog_ref[...]) * softplus
    w_off, b_off, a_off = d_idx, d_idx + h_idx, d_idx + h_idx + h_b
    out = jnp.where(isk, kn,
                    jnp.where(lane < b_off, wv,
                              jnp.where(lane < a_off, beta,
                                        jnp.where(lane < a_off + h_b, la, 0.0))))
    sm_ref[...] = out
    ki_ref[...] = kn[:, :d_idx].astype(BF)


def idx_proj(xb, w, lng, lnb, alog, dtb, l, tm, d_idx, h_idx, h_b):
    M, D = xb.shape
    N = w.shape[2]
    n_qi = N - LANES
    vec = pl.BlockSpec((1, LANES), lambda i: (0, 0))
    return pl.pallas_call(
        functools.partial(_idx_kernel, n_qi=n_qi, d_idx=d_idx, h_idx=h_idx, h_b=h_b),
        grid=(M // tm,),
        in_specs=[pl.BlockSpec((tm, D), lambda i: (i, 0)),
                  pl.BlockSpec((None, D, N), lambda i: (l, 0, 0)),
                  vec, vec, vec, vec],
        out_specs=[pl.BlockSpec((tm, n_qi), lambda i: (i, 0)),
                   pl.BlockSpec((tm, d_idx), lambda i: (i, 0)),
                   pl.BlockSpec((tm, LANES), lambda i: (i, 0))],
        out_shape=[jax.ShapeDtypeStruct((M, n_qi), BF),
                   jax.ShapeDtypeStruct((M, d_idx), BF),
                   jax.ShapeDtypeStruct((M, LANES), F32)],
        compiler_params=_cparams(("parallel",)),
        name="idx_proj",
    )(xb, w, lng, lnb, alog, dtb)


def _sort_key(s):
    b = pltpu.bitcast(s + 0.0, I32)
    return b ^ ((b >> 31) & 0x7FFFFFFF)


def _transpose_to(dst_ref, x):
    tq = x.shape[0]
    if tq % LANES:
        x = jnp.concatenate([x, jnp.zeros((LANES - tq % LANES, x.shape[1]), F32)], axis=0)
    dst_ref[...] = x.T[:, :tq].astype(dst_ref.dtype)


def _index_scores(ki_chunk, qit_ref, w_rows, h_idx, d_idx):
    s = [jnp.dot(ki_chunk, qit_ref[h * d_idx:(h + 1) * d_idx, :], preferred_element_type=F32)
         for h in range(h_idx)]
    acc = w_rows[0:1, :] * jnp.maximum(s[0], 0.0)
    for h in range(1, h_idx):
        acc = acc + w_rows[h:h + 1, :] * jnp.maximum(s[h], 0.0)
    return acc


def _store_key(hi_ref, lo_ref, where, key):
    hi_ref[where] = (key >> 16).astype(I16)
    lo_ref[where] = (key ^ I16_OFF).astype(I16)


def _count16(nck, tq, pred):
    def body(i, cnt):
        ones = (jnp.where(pred(2 * i), jnp.int16(1), jnp.int16(0))
                + jnp.where(pred(2 * i + 1), jnp.int16(1), jnp.int16(0)))
        parts = [ones[r * PACK16:(r + 1) * PACK16] for r in range(ones.shape[0] // PACK16)]
        while len(parts) > 1:
            parts = [a + b for a, b in zip(parts[::2], parts[1::2])]
        return cnt + parts[0]
    cnt16 = lax.fori_loop(0, (nck + 1) // 2, body, jnp.zeros((PACK16, tq), I16))
    return cnt16.astype(I32).sum(axis=0, keepdims=True)


def _mask_pad_chunk(hi_ref, lo_ref, nck):
    if isinstance(nck, int):
        if nck % 2:
            hi_ref[nck] = jnp.full(hi_ref.shape[1:], -I16_OFF, I16)
            lo_ref[nck] = jnp.full(lo_ref.shape[1:], -I16_OFF, I16)
        return

    @pl.when(nck % 2 == 1)
    def _():
        hi_ref[nck] = jnp.full(hi_ref.shape[1:], -I16_OFF, I16)
        lo_ref[nck] = jnp.full(lo_ref.shape[1:], -I16_OFF, I16)


def _search16(nck, tq, n_sel, base, cnt0, count_ge):
    def bit_body(i, carry):
        thr, cnt_thr = carry
        cand = thr + jnp.left_shift(jnp.int32(1), 15 - i)
        cnt = base + count_ge(cand.astype(I16))
        ok = cnt >= n_sel
        return jnp.where(ok, cand, thr), jnp.where(ok, cnt, cnt_thr)
    return lax.fori_loop(0, 16, bit_body, (jnp.full((1, tq), -I16_OFF, I32), cnt0))


def _select_top(hi_ref, lo_ref, nck, n_sel, tq, idx_bits, cut_ref):
    kc = hi_ref.shape[1]
    zero = jnp.zeros((1, tq), I32)
    _mask_pad_chunk(hi_ref, lo_ref, nck)
    thr_hi, cnt_ge_hi = _search16(nck, tq, n_sel, zero, zero,
                                  lambda t: _count16(nck, tq, lambda c: hi_ref[c] >= t))
    t_hi = thr_hi.astype(I16)
    cnt_gt_hi = _count16(nck, tq, lambda c: hi_ref[c] > t_hi)

    def lo_body(c, carry):
        lo_ref[c] = jnp.where(hi_ref[c] == t_hi, lo_ref[c], jnp.int16(-I16_OFF))
        return carry
    lax.fori_loop(0, nck, lo_body, 0)
    thr_lo, cnt_thr = _search16(nck, tq, n_sel, cnt_gt_hi, cnt_ge_hi,
                                lambda t: _count16(nck, tq, lambda c: lo_ref[c] >= t))
    t_lo = thr_lo.astype(I16)
    has_tie = (thr_hi > -I16_OFF) & (cnt_thr > n_sel)
    cut_ref[...] = jnp.full((1, tq), I16_OFF - 1, I32)

    def kidx16(c):
        return (c * kc + lax.broadcasted_iota(I32, (kc, tq), 0)).astype(I16)

    @pl.when(jnp.max(has_tie.astype(I32)) > 0)
    def _():
        need = n_sel - cnt_gt_hi - _count16(nck, tq, lambda c: (hi_ref[c] == t_hi) & (lo_ref[c] > t_lo))

        def idx_body(i, cut):
            cand = cut + jnp.left_shift(jnp.int32(1), idx_bits - 1 - i)
            c16 = cand.astype(I16)
            cnt = _count16(nck, tq, lambda c: (hi_ref[c] == t_hi) & (lo_ref[c] == t_lo) & (kidx16(c) < c16))
            return jnp.where(cnt < need, cand, cut)

        cut = lax.fori_loop(0, idx_bits, idx_body, zero)
        cut_ref[...] = jnp.where(has_tie, cut, I16_OFF - 1)

    cut = cut_ref[...].astype(I16)

    def bias_body(c, carry):
        hi, lo = hi_ref[c], lo_ref[c]
        sel = (hi > t_hi) | ((hi == t_hi) & ((lo > t_lo) | ((lo == t_lo) & (kidx16(c) <= cut))))
        sel = sel & (hi > jnp.int16(-I16_OFF))
        hi_ref[c] = pltpu.bitcast(jnp.where(sel, jnp.zeros((), BF), jnp.full((), FAR, BF)), I16)
        return carry
    lax.fori_loop(0, nck, bias_body, 0)


def _flash_scores(slot, far16, dist, k_heads, qt_ref, m_ref, lt_ref, p_ref, a_ref, n_heads, hd):
    kc = dist.shape[0]
    dist = dist + pltpu.bitcast(far16, BF).astype(F32)
    for h in range(n_heads):
        cols = slice(h * hd, (h + 1) * hd)
        lt_ref[h, 0:kc, :] = jnp.dot(k_heads[h](), qt_ref[cols, :], preferred_element_type=F32)
    for h in range(n_heads):
        slope = float(2.0 ** (-(8.0 / n_heads) * (h + 1)) * LOG2E)
        lt = lt_ref[h, 0:kc, :] - slope * dist
        m_old = m_ref[h]
        m_new = jnp.maximum(m_old, jnp.max(lt, axis=0, keepdims=True))
        p = jnp.exp2(lt - m_new)
        a = jnp.exp2(m_old - m_new)
        p_ref[slot, h, 0:kc, :] = p.astype(BF)
        a_ref[slot, h] = a
        m_ref[h] = m_new


def _flash_values(slot, kc, vt_heads, l_ref, acc_ref, p_ref, a_ref, n_heads, hd):
    for h in range(n_heads):
        pv = jnp.dot(vt_heads[h](), p_ref[slot, h, 0:kc, :], preferred_element_type=F32)
        a = a_ref[slot, h]
        acc_ref[h] = a * acc_ref[h] + pv[:hd]
        l_ref[h] = a * l_ref[h] + pv[hd:hd + 1]


def _flash_init(m_ref, l_ref, acc_ref):
    m_ref[...] = jnp.full(m_ref.shape, NEG, F32)
    l_ref[...] = jnp.zeros(l_ref.shape, F32)
    acc_ref[...] = jnp.zeros(acc_ref.shape, F32)


def _flash_finish(o_ref, l_ref, acc_ref, n_heads, hd, tq):
    for h in range(n_heads):
        ot = acc_ref[h] / l_ref[h]
        if tq % LANES:
            ot = jnp.concatenate([ot, jnp.zeros((hd, LANES - tq % LANES), F32)], axis=1)
        o_ref[:, h * hd:(h + 1) * hd] = ot.T[:tq].astype(BF)


def _dsa_prompt_kernel(qmap, kmap, nckq, nfull, q_ref, k_ref, vt_ref, qi_ref, ki_ref, smt_ref, o_ref,
                       hi_ref, lo_ref, cut_ref, qt_ref, qit_ref, w_ref, m_ref, l_ref, acc_ref, lt_ref, p_ref, a_ref,
                       *, n_heads, hd, h_idx, d_idx, n_meta, t_valid, n_sel, cpb, idx_bits):
    p = pl.program_id(0)
    qb, kb = qmap[p], kmap[p]
    nck = nckq[qb]
    tq = q_ref.shape[0]
    qpos = qb * tq + lax.broadcasted_iota(I32, (KC, tq), 1)

    @pl.when(kb == 0)
    def _():
        _transpose_to(qt_ref, q_ref[...] * (hd ** -0.5 * LOG2E))
        _transpose_to(qit_ref, qi_ref[...].astype(F32))
        w_ref[...] = smt_ref[d_idx:d_idx + SUBLANES, :] * (d_idx ** -0.5)
        qchk = jnp.maximum((qpos - n_meta) >> 6, -1)

        def score_body(c, carry, masked):
            k0 = pl.multiple_of(c * KC, KC)
            key = _sort_key(_index_scores(ki_ref[pl.ds(k0, KC), :], qit_ref, w_ref[...], h_idx, d_idx))
            if masked:
                kpos = k0 + lax.broadcasted_iota(I32, (KC, tq), 0)
                allowed = (jnp.maximum((kpos - n_meta) >> 6, -1) <= qchk) & (kpos < t_valid)
                key = jnp.where(allowed, key, INT_MIN)
            _store_key(hi_ref, lo_ref, c, key)
            return carry
        lax.fori_loop(0, nfull[qb], functools.partial(score_body, masked=False), 0)
        lax.fori_loop(nfull[qb], nck, functools.partial(score_body, masked=True), 0)
        _select_top(hi_ref, lo_ref, nck, n_sel, tq, idx_bits, cut_ref)
        _flash_init(m_ref, l_ref, acc_ref)

    vr = vt_ref.shape[1] // n_heads

    def chunk_body(c, carry):
        cg = kb * cpb + c
        rows = pl.ds(pl.multiple_of(c * KC, KC), KC)
        kpos = cg * KC + lax.broadcasted_iota(I32, (KC, tq), 0)
        dist = jnp.abs(qpos - kpos).astype(F32)
        k_heads = [functools.partial(lambda h: k_ref[rows, h * hd:(h + 1) * hd], h) for h in range(n_heads)]
        vt_heads = [functools.partial(lambda h: vt_ref[c, h * vr:(h + 1) * vr, :], h) for h in range(n_heads)]
        _flash_scores(0, hi_ref[cg], dist, k_heads, qt_ref, m_ref, lt_ref, p_ref, a_ref, n_heads, hd)
        _flash_values(0, KC, vt_heads, l_ref, acc_ref, p_ref, a_ref, n_heads, hd)
        return carry
    lax.fori_loop(0, jnp.minimum(cpb, nck - kb * cpb), chunk_body, 0)

    @pl.when((kb + 1) * cpb >= nck)
    def _():
        _flash_finish(o_ref, l_ref, acc_ref, n_heads, hd, tq)


def dsa_prompt(proj, kb16, vt3, qi, ki, smt, *, t_valid, n_meta, n_heads, hd, h_idx, d_idx, n_sel, tq, tk):
    Mp = proj.shape[0]
    C = n_heads * hd
    vr = vt3.shape[1]
    cpb = tk // KC
    nq = Mp // tq
    qlast = np.minimum(np.arange(nq) * tq + tq - 1, t_valid - 1)
    kend = np.minimum(t_valid, n_meta + CHUNK * (np.maximum((qlast - n_meta) // CHUNK, -1) + 1))
    nckq = -(-kend // KC)
    qfirst = np.arange(nq) * tq
    kend_first = np.minimum(t_valid, n_meta + CHUNK * (np.maximum((qfirst - n_meta) // CHUNK, -1) + 1))
    nfull = np.minimum(kend_first // KC, nckq)
    pairs = [(i, j) for i in range(nq) for j in range(-(-int(nckq[i]) // cpb))]
    qmap = jnp.asarray([a for a, _ in pairs], I32)
    kmap = jnp.asarray([b for _, b in pairs], I32)
    idx_bits = max(1, int(np.ceil(np.log2(Mp))))
    grid_spec = pltpu.PrefetchScalarGridSpec(
        num_scalar_prefetch=4,
        grid=(len(pairs),),
        in_specs=[
            pl.BlockSpec((tq, C), lambda p, qm, km, nk, nf: (qm[p], 0)),
            pl.BlockSpec((tk, C), lambda p, qm, km, nk, nf: (km[p], 0)),
            pl.BlockSpec((cpb, vr, KC), lambda p, qm, km, nk, nf: (km[p], 0, 0)),
            pl.BlockSpec((tq, h_idx * d_idx), lambda p, qm, km, nk, nf: (qm[p], 0)),
            pl.BlockSpec((Mp, d_idx), lambda p, qm, km, nk, nf: (0, 0)),
            pl.BlockSpec((LANES, tq), lambda p, qm, km, nk, nf: (0, qm[p])),
        ],
        out_specs=pl.BlockSpec((tq, C), lambda p, qm, km, nk, nf: (qm[p], 0)),
        scratch_shapes=[
            pltpu.VMEM((Mp // KC + 1, KC, tq), I16), pltpu.VMEM((Mp // KC + 1, KC, tq), I16),
            pltpu.VMEM((1, tq), I32),
            pltpu.VMEM((C, tq), BF), pltpu.VMEM((h_idx * d_idx, tq), BF), pltpu.VMEM((SUBLANES, tq), F32),
            pltpu.VMEM((n_heads, 1, tq), F32), pltpu.VMEM((n_heads, 1, tq), F32),
            pltpu.VMEM((n_heads, hd, tq), F32),
            pltpu.VMEM((n_heads, KC, tq), F32), pltpu.VMEM((2, n_heads, KC, tq), BF),
            pltpu.VMEM((2, n_heads, 1, tq), F32),
        ],
    )
    return pl.pallas_call(
        functools.partial(_dsa_prompt_kernel, n_heads=n_heads, hd=hd, h_idx=h_idx, d_idx=d_idx, n_meta=n_meta,
                          t_valid=t_valid, n_sel=n_sel, cpb=cpb, idx_bits=idx_bits),
        grid_spec=grid_spec,
        out_shape=jax.ShapeDtypeStruct((Mp, C), BF),
        compiler_params=_cparams(("arbitrary",)),
        name="dsa_prompt",
    )(qmap, kmap, jnp.asarray(nckq, I32), jnp.asarray(nfull, I32), proj, kb16, vt3, qi, ki, smt)


def _dsa_sample_kernel(q_ref, kn_ref, vn_ref, kc_ref, vc_ref, qi_ref, kin_ref, kic_ref, smt_ref, o_ref,
                       hi_ref, lo_ref, cut_ref, qt_ref, qit_ref, w_ref, m_ref, l_ref, acc_ref, lt_ref, p_ref, a_ref,
                       *, n_heads, hd, h_idx, d_idx, past, n_sel, cpb, nkb, idx_bits):
    kb = pl.program_id(1)
    tq = q_ref.shape[0]
    ncache = past // KC
    nck = ncache + 1
    qpos = past + lax.broadcasted_iota(I32, (KC, tq), 1)

    @pl.when(kb == 0)
    def _():
        _transpose_to(qt_ref, q_ref[...] * (hd ** -0.5 * LOG2E))
        _transpose_to(qit_ref, qi_ref[...].astype(F32))
        w_ref[...] = smt_ref[d_idx:d_idx + SUBLANES, :] * (d_idx ** -0.5)

        def score_body(c, carry):
            k0 = pl.multiple_of(c * KC, KC)
            sc = _index_scores(kic_ref[pl.ds(k0, KC), :], qit_ref, w_ref[...], h_idx, d_idx)
            _store_key(hi_ref, lo_ref, c, _sort_key(sc))
            return carry
        lax.fori_loop(0, ncache, score_body, 0)
        hi_ref[ncache] = jnp.full((KC, tq), -I16_OFF, I16)
        lo_ref[ncache] = jnp.full((KC, tq), -I16_OFF, I16)
        sc = _index_scores(kin_ref[...], qit_ref, w_ref[...], h_idx, d_idx)
        _store_key(hi_ref, lo_ref, (ncache, slice(0, tq), slice(None)), _sort_key(sc))
        _select_top(hi_ref, lo_ref, nck, n_sel, tq, idx_bits, cut_ref)
        _flash_init(m_ref, l_ref, acc_ref)

    def vt_of(v):
        kc = v.shape[0]
        if kc % LANES:
            v = jnp.concatenate([v, jnp.zeros((LANES - kc % LANES, hd), F32)], axis=0)
        return jnp.concatenate([v.T[:, :kc].astype(BF), jnp.ones((ONES_ROWS, kc), BF)], axis=0)

    vall = [None, None]
    for c in range(cpb):
        cg = kb * cpb + c
        kall = jnp.swapaxes(kc_ref[c * KC:(c + 1) * KC], 0, 1)
        kpos = cg * KC + lax.broadcasted_iota(I32, (KC, tq), 0)
        dist = jnp.abs(qpos - kpos).astype(F32)
        k_heads = [functools.partial(lambda h, kall=kall: kall[h].astype(BF), h) for h in range(n_heads)]
        if c > 0:
            vt_heads = [functools.partial(lambda h, v=vall[(c - 1) % 2]: vt_of(v[h]), h) for h in range(n_heads)]
            _flash_values((c - 1) % 2, KC, vt_heads, l_ref, acc_ref, p_ref, a_ref, n_heads, hd)
        vall[c % 2] = jnp.swapaxes(vc_ref[c * KC:(c + 1) * KC], 0, 1)
        _flash_scores(c % 2, hi_ref[cg], dist, k_heads, qt_ref, m_ref, lt_ref, p_ref, a_ref, n_heads, hd)
    vt_heads = [functools.partial(lambda h, v=vall[(cpb - 1) % 2]: vt_of(v[h]), h) for h in range(n_heads)]
    _flash_values((cpb - 1) % 2, KC, vt_heads, l_ref, acc_ref, p_ref, a_ref, n_heads, hd)

    @pl.when(kb == nkb - 1)
    def _():
        tn = kn_ref.shape[0]
        kpos = past + lax.broadcasted_iota(I32, (tn, tq), 0)
        dist = jnp.abs(past + lax.broadcasted_iota(I32, (tn, tq), 1) - kpos).astype(F32)
        k_heads = [functools.partial(lambda h: kn_ref[:, h * hd:(h + 1) * hd].astype(BF), h) for h in range(n_heads)]
        vt_heads = [functools.partial(lambda h: vt_of(vn_ref[:, h * hd:(h + 1) * hd]), h) for h in range(n_heads)]
        _flash_scores(0, hi_ref[ncache, 0:tn, :], dist, k_heads, qt_ref, m_ref, lt_ref, p_ref, a_ref, n_heads, hd)
        _flash_values(0, tn, vt_heads, l_ref, acc_ref, p_ref, a_ref, n_heads, hd)
        _flash_finish(o_ref, l_ref, acc_ref, n_heads, hd, tq)


def dsa_sample(proj, kcache, vcache, qi, ki, kicache, smt, l, *, n_batch, t_new, n_heads, hd, h_idx, d_idx, n_sel, tk):
    C = n_heads * hd
    past = kcache.shape[2]
    cpb = tk // KC
    nkb = past // tk
    idx_bits = max(1, int(np.ceil(np.log2(past + KC))))
    tq = t_new
    return pl.pallas_call(
        functools.partial(_dsa_sample_kernel, n_heads=n_heads, hd=hd, h_idx=h_idx, d_idx=d_idx, past=past,
                          n_sel=n_sel, cpb=cpb, nkb=nkb, idx_bits=idx_bits),
        grid=(n_batch, nkb),
        in_specs=[
            pl.BlockSpec((tq, C), lambda b, j: (b, 0)),
            pl.BlockSpec((tq, C), lambda b, j: (b, 1)),
            pl.BlockSpec((tq, C), lambda b, j: (b, 2)),
            pl.BlockSpec((None, None, tk, n_heads, hd), lambda b, j: (l, b, j, 0, 0)),
            pl.BlockSpec((None, None, tk, n_heads, hd), lambda b, j: (l, b, j, 0, 0)),
            pl.BlockSpec((tq, h_idx * d_idx), lambda b, j: (b, 0)),
            pl.BlockSpec((tq, d_idx), lambda b, j: (b, 0)),
            pl.BlockSpec((None, None, past, d_idx), lambda b, j: (l, b, 0, 0)),
            pl.BlockSpec((None, LANES, tq), lambda b, j: (b, 0, 0)),
        ],
        out_specs=pl.BlockSpec((tq, C), lambda b, j: (b, 0)),
        out_shape=jax.ShapeDtypeStruct((n_batch * tq, C), BF),
        scratch_shapes=[
            pltpu.VMEM((past // KC + 2, KC, tq), I16), pltpu.VMEM((past // KC + 2, KC, tq), I16),
            pltpu.VMEM((1, tq), I32),
            pltpu.VMEM((C, tq), BF), pltpu.VMEM((h_idx * d_idx, tq), BF), pltpu.VMEM((SUBLANES, tq), F32),
            pltpu.VMEM((n_heads, 1, tq), F32), pltpu.VMEM((n_heads, 1, tq), F32),
            pltpu.VMEM((n_heads, hd, tq), F32),
            pltpu.VMEM((n_heads, KC, tq), F32), pltpu.VMEM((2, n_heads, KC, tq), BF),
            pltpu.VMEM((2, n_heads, 1, tq), F32),
        ],
        compiler_params=_cparams(("parallel", "arbitrary")),
        name="dsa_sample",
    )(proj, proj, proj, kcache, vcache, qi, ki, kicache, smt)


def _split3(x):
    hi = x.astype(BF)
    r = x - hi.astype(F32)
    mid = r.astype(BF)
    lo = (r - mid.astype(F32)).astype(BF)
    return hi, mid, lo


def _gdn_kernel(x_ref, gb_ref, sm_ref, cp_ref, s0_ref, cw_ref, ng_ref, ob_ref, sn_ref, S_ref, cbuf_ref,
                *, n_heads, hd, t_valid, conv_w, b_off, a_off, nchunks):
    c = pl.program_id(1)
    C = x_ref.shape[0]
    cd = n_heads * hd
    halo = conv_w - 1

    @pl.when(c == 0)
    def _():
        S_ref[...] = s0_ref[...]
        cbuf_ref[SUBLANES - halo:SUBLANES, :] = cp_ref[...]

    cbuf_ref[SUBLANES:SUBLANES + C, :] = x_ref[...]
    conv = None
    for i in range(conv_w):
        term = cbuf_ref[SUBLANES - halo + i:SUBLANES - halo + i + C, :] * cw_ref[i:i + 1, :]
        conv = term if conv is None else conv + term
    cbuf_ref[0:SUBLANES, :] = cbuf_ref[C:C + SUBLANES, :]
    act = conv * _sigmoid(conv)

    sm = sm_ref[...]
    row = c * C + lax.broadcasted_iota(I32, (C, LANES), 0)
    lane = lax.broadcasted_iota(I32, (C, LANES), 1)
    valid = row < t_valid
    la = jnp.where(valid & (lane >= a_off) & (lane < a_off + n_heads), sm, 0.0)
    beta_t = jnp.where(valid, sm, 0.0)
    ri = lax.broadcasted_iota(I32, (C, C), 0)
    ci = lax.broadcasted_iota(I32, (C, C), 1)
    causal = ri >= ci
    strict = ri > ci
    ltri = jnp.where(causal, 1.0, 0.0).astype(BF)
    g_t = sum(jnp.dot(ltri, piece, preferred_element_type=F32) for piece in _split3(la))
    g_rows = jnp.concatenate([g_t, jnp.zeros((LANES - C, LANES), F32)], axis=0).T

    H = range(n_heads)
    dot = functools.partial(jnp.dot, preferred_element_type=F32)
    dot_nt = functools.partial(lax.dot_general, dimension_numbers=NT_DIMS, preferred_element_type=F32)
    qh = [act[:, h * hd:(h + 1) * hd] for h in H]
    kh = [act[:, cd + h * hd:cd + (h + 1) * hd] for h in H]
    vh = [act[:, 2 * cd + h * hd:2 * cd + (h + 1) * hd] for h in H]
    qn = [qh[h] * lax.rsqrt(jnp.sum(qh[h] * qh[h], axis=-1, keepdims=True) + NORM_EPS) * (hd ** -0.5) for h in H]
    kn = [kh[h] * lax.rsqrt(jnp.sum(kh[h] * kh[h], axis=-1, keepdims=True) + NORM_EPS) for h in H]
    beta = [beta_t[:, b_off + h:b_off + h + 1] for h in H]
    g_c = [g_t[:, a_off + h:a_off + h + 1] for h in H]
    g_r = [g_rows[a_off + h:a_off + h + 1, :C] for h in H]
    g_last = [g_t[C - 1:C, a_off + h:a_off + h + 1] for h in H]
    decay = [jnp.where(causal, jnp.exp(jnp.where(causal, g_c[h] - g_r[h], 0.0)), 0.0) for h in H]
    eg = [jnp.exp(g_c[h]) for h in H]
    kb = [kn[h].astype(BF) for h in H]
    qb = [qn[h].astype(BF) for h in H]
    Sb = [S_ref[h].astype(BF) for h in H]
    kk = [dot_nt(kb[h], kb[h]) for h in H]
    qk = [dot_nt(qb[h], kb[h]) * decay[h] for h in H]
    o_state = [dot((qn[h] * eg[h]).astype(BF), Sb[h]) for h in H]
    N = [jnp.where(strict, -(beta[h] * kk[h] * decay[h]), 0.0) for h in H]
    P = list(N)
    for _ in range(int(np.log2(C)) - 1):
        Pb = [P[h].astype(BF) for h in H]
        P = [dot(Pb[h], Pb[h]) for h in H]
        NP = [dot(N[h].astype(BF), P[h].astype(BF)) for h in H]
        N = [N[h] + P[h] + NP[h] for h in H]
    rhs = [jnp.concatenate([vh[h] * beta[h], kn[h] * (beta[h] * eg[h])], axis=1) for h in H]
    corr = [dot(N[h].astype(BF), rhs[h].astype(BF)) for h in H]
    sol = [rhs[h] + corr[h] for h in H]
    su = [dot(sol[h][:, hd:].astype(BF), Sb[h]) for h in H]
    ub = [(sol[h][:, :hd] - su[h]).astype(BF) for h in H]
    o = [o_state[h] + dot(qk[h].astype(BF), ub[h]) for h in H]
    kd = [(kn[h] * jnp.exp(g_last[h] - g_c[h])).astype(BF) for h in H]
    upd = [lax.dot_general(kd[h], ub[h], TN_DIMS, preferred_element_type=F32) for h in H]
    for h in H:
        cols = slice(h * hd, (h + 1) * hd)
        S_ref[h] = S_ref[h] * jnp.exp(g_last[h]) + upd[h]
        gh = gb_ref[:, cols]
        on = o[h] * lax.rsqrt(jnp.mean(o[h] * o[h], axis=-1, keepdims=True) + NORM_EPS) * ng_ref[...]
        ob_ref[:, cols] = (on * (gh * _sigmoid(gh))).astype(BF)

    @pl.when(c == nchunks - 1)
    def _():
        sn_ref[...] = S_ref[...]


def gdn(proj, sm, conv_prev, s0, cw, ng, *, n_batch, rows_per_batch, t_valid, n_heads, hd, col0, b_off, a_off):
    cd = n_heads * hd
    nchunks = rows_per_batch // DN_C
    conv_w = cw.shape[0]
    qkv_blk = col0 // (3 * cd)
    g_blk = (col0 + 3 * cd) // cd
    return pl.pallas_call(
        functools.partial(_gdn_kernel, n_heads=n_heads, hd=hd, t_valid=t_valid, conv_w=conv_w,
                          b_off=b_off, a_off=a_off, nchunks=nchunks),
        grid=(n_batch, nchunks),
        in_specs=[
            pl.BlockSpec((DN_C, 3 * cd), lambda b, c: (b * nchunks + c, qkv_blk)),
            pl.BlockSpec((DN_C, cd), lambda b, c: (b * nchunks + c, g_blk)),
            pl.BlockSpec((DN_C, LANES), lambda b, c: (b * nchunks + c, 0)),
            pl.BlockSpec((None, conv_w - 1, 3 * cd), lambda b, c: (b, 0, 0)),
            pl.BlockSpec((None, n_heads, hd, hd), lambda b, c: (b, 0, 0, 0)),
            pl.BlockSpec((conv_w, 3 * cd), lambda b, c: (0, 0)),
            pl.BlockSpec((1, hd), lambda b, c: (0, 0)),
        ],
        out_specs=[pl.BlockSpec((DN_C, cd), lambda b, c: (b * nchunks + c, 0)),
                   pl.BlockSpec((None, n_heads, hd, hd), lambda b, c: (b, 0, 0, 0))],
        out_shape=[jax.ShapeDtypeStruct((n_batch * rows_per_batch, cd), BF),
                   jax.ShapeDtypeStruct((n_batch, n_heads, hd, hd), F32)],
        scratch_shapes=[pltpu.VMEM((n_heads, hd, hd), F32),
                        pltpu.VMEM((DN_C + SUBLANES, 3 * cd), F32)],
        compiler_params=_cparams(("parallel", "arbitrary")),
        name="gdn",
    )(proj, proj, sm, conv_prev, s0, cw, ng)


def _merge_kernel(h_ref, hb_ref, oa_ref, ob_ref, wga_ref, wgb_ref, woa_ref, wob_ref, wout_ref, g_ref, b_ref,
                  o_ref, acc_ref, *, alpha, nn, rs):
    j = pl.program_id(1)

    @pl.when(j == 0)
    def _():
        acc_ref[...] = jnp.zeros_like(acc_ref)

    hb = hb_ref[...]
    ga = _sigmoid(jnp.dot(hb, wga_ref[...], preferred_element_type=F32))
    gb = _sigmoid(jnp.dot(hb, wgb_ref[...], preferred_element_type=F32))
    y = (ga * jnp.dot(oa_ref[...], woa_ref[...], preferred_element_type=F32)
         + gb * jnp.dot(ob_ref[...], wob_ref[...], preferred_element_type=F32))
    acc_ref[...] += jnp.dot(y.astype(BF), wout_ref[...], preferred_element_type=F32)

    @pl.when(j == nn - 1)
    def _():
        def slab(r, carry):
            rows = pl.ds(pl.multiple_of(r * rs, rs), rs)
            o_ref[rows, :] = _layer_norm(alpha * h_ref[rows, :] + acc_ref[rows, :], g_ref[...], b_ref[...])
            return carry
        lax.fori_loop(0, h_ref.shape[0] // rs, slab, 0)


def merge_ln(h, hb, oa, ob, wgate, woa, wob, wout, g, b, l, alpha, tm, tn):
    M, D = h.shape
    ca, cb = oa.shape[1], ob.shape[1]
    nn = D // tn
    rs = 64 if tm % 64 == 0 else 16
    return pl.pallas_call(
        functools.partial(_merge_kernel, alpha=alpha, nn=nn, rs=rs),
        grid=(M // tm, nn),
        in_specs=[
            pl.BlockSpec((tm, D), lambda i, j: (i, 0)),
            pl.BlockSpec((tm, D), lambda i, j: (i, 0)),
            pl.BlockSpec((tm, ca), lambda i, j: (i, 0)),
            pl.BlockSpec((tm, cb), lambda i, j: (i, 0)),
            pl.BlockSpec((None, D, tn), lambda i, j: (l, 0, j)),
            pl.BlockSpec((None, D, tn), lambda i, j: (l, 0, nn + j)),
            pl.BlockSpec((None, ca, tn), lambda i, j: (l, 0, j)),
            pl.BlockSpec((None, cb, tn), lambda i, j: (l, 0, j)),
            pl.BlockSpec((None, tn, D), lambda i, j: (l, j, 0)),
            pl.BlockSpec((1, D), lambda i, j: (0, 0)),
            pl.BlockSpec((1, D), lambda i, j: (0, 0)),
        ],
        out_specs=pl.BlockSpec((tm, D), lambda i, j: (i, 0)),
        out_shape=jax.ShapeDtypeStruct((M, D), F32),
        scratch_shapes=[pltpu.VMEM((tm, D), F32)],
        compiler_params=_cparams(("parallel", "arbitrary")),
        name="merge_ln",
    )(h, hb, oa, ob, wgate, wgate, woa, wob, wout, g, b)


def _round_up(n, m):
    return -(-n // m) * m


def kernel(x_prompt, x_sample, cache_k, cache_v, cache_kidx, state_conv, state_dn, meta, w_ff_gu, w_ff_down, ln_g, ln_b, w_in, conv_w, idx_ln_g, idx_ln_b, a_log, dt_bias, dn_norm_g, w_oa, w_ob, w_out):
    depth = w_in.shape[0]
    B, S, D = x_prompt.shape
    Bs, Ts, _ = x_sample.shape
    n_meta = meta.shape[0]
    past, HA, HDA = cache_k.shape[2], cache_k.shape[3], cache_k.shape[4]
    CA = HA * HDA
    d_idx = cache_kidx.shape[-1]
    HB, HDB = state_dn.shape[2], state_dn.shape[3]
    CD = HB * HDB
    cw_len = conv_w.shape[1]
    F = w_ff_down.shape[2]
    n_in = w_in.shape[2]
    h_idx = (n_in - 3 * CA - d_idx - 4 * CD - 2 * HB - 2 * D) // (d_idx + 1)
    assert B == 1 and Ts == DN_C and CHUNK == 64 and n_meta <= CHUNK
    assert d_idx + h_idx + 2 * HB <= LANES and h_idx <= SUBLANES and HA * HDA == CA
    assert past % KC == 0 and CA == CD
    alpha = float((2 * depth) ** 0.25)
    T = n_meta + S
    Mp = _round_up(T, ROW_PAD)
    n_sel_p = min(MAX_SEL, S // 4)
    n_sel_s = min(MAX_SEL, (past + Ts) // 4)
    w_off, b_off, a_off = d_idx, d_idx + h_idx, d_idx + h_idx + HB

    Fp = _round_up(F, FF_ALIGN)
    padf = ((0, 0), (0, 0), (0, 0), (0, Fp - F))
    wg = jnp.pad(w_ff_gu[..., :F].astype(BF), padf)
    wu = jnp.pad(w_ff_gu[..., F:].astype(BF), padf)
    wd = jnp.pad(w_ff_down.astype(BF), ((0, 0), (0, 0), (0, Fp - F), (0, 0)))
    offs = np.cumsum([0, CA, CA, CA, h_idx * d_idx, d_idx, h_idx, 3 * CD, CD, HB, HB, D, D])
    span = lambda i, j: w_in[:, :, offs[i]:offs[j]].astype(BF)
    w_main = jnp.concatenate([span(0, 3), span(6, 8)], axis=2)
    pad_small = LANES - (d_idx + h_idx + 2 * HB)
    w_idx = jnp.concatenate([span(3, 6), span(8, 10), jnp.zeros((depth, D, pad_small), BF)], axis=2)
    w_gate = span(10, 12)
    woa, wob, wout = w_oa.astype(BF), w_ob.astype(BF), w_out.astype(BF)

    def lane_vec(v, off):
        return jnp.zeros((depth, 1, LANES), F32).at[:, 0, off:off + v.shape[1]].set(v)
    lng_v, lnb_v = lane_vec(idx_ln_g, 0), lane_vec(idx_ln_b, 0)
    alog_v, dtb_v = lane_vec(a_log, a_off), lane_vec(dt_bias, a_off)

    xp = jnp.concatenate([meta.astype(F32), x_prompt[0], jnp.zeros((Mp - T, D), F32)], axis=0)
    xs = x_sample.reshape(Bs * Ts, D)
    Ms = Bs * Ts
    tm_p, tm_s = _pick_tile(Mp, TM_FFN), _pick_tile(Ms, 512)
    tmm_p, tmm_s = _pick_tile(Mp, TM_MM), _pick_tile(Ms, 1024)
    tf = FF_ALIGN
    tn_mm = _pick_tile(w_main.shape[2], TN_MM)
    tn_mg = _pick_tile(D, TN_MERGE)
    tq = _pick_tile(Mp, TQ)
    tk_p = _pick_tile(Mp, TK_PROMPT)
    tk_c = _pick_tile(past, TK_CACHE)
    assert tq % LANES == 0 and tk_p % KC == 0 and tk_c % KC == 0 and Mp % KC == 0

    kcache, vcache = cache_k, cache_v
    kicache = cache_kidx.astype(BF)
    conv0 = jnp.zeros((1, cw_len - 1, 3 * CD), F32)
    s0 = jnp.zeros((1, HB, HDB, HDB), F32)

    outs = [[] for _ in range(10)]
    for l in range(depth):
        g3 = [ln_g[l, i][None] for i in range(3)]
        b3 = [ln_b[l, i][None] for i in range(3)]
        ng = dn_norm_g[l][None]
        streams = []
        for (x, is_prompt) in ((xp, True), (xs, False)):
            tm, tmm = (tm_p, tmm_p) if is_prompt else (tm_s, tmm_s)
            h, hb = ffn_ln(x, wg, wu, wd, g3[0], b3[0], l, 0, alpha, tm, tf)
            proj = matmul(hb, w_main, l, tmm, tn_mm)
            qi, ki, sm = idx_proj(hb, w_idx, lng_v[l], lnb_v[l], alog_v[l], dtb_v[l], l, tmm, d_idx, h_idx, HB)
            if is_prompt:
                kb16 = proj[:, CA:2 * CA].astype(BF)
                vt4 = proj[:, 2 * CA:3 * CA].astype(BF).reshape(Mp // KC, KC, HA, HDA).transpose(0, 2, 3, 1)
                vt3 = jnp.concatenate([vt4, jnp.ones((Mp // KC, HA, ONES_ROWS, KC), BF)], axis=2)
                vt3 = vt3.reshape(Mp // KC, HA * (HDA + ONES_ROWS), KC)
                oa = dsa_prompt(proj, kb16, vt3, qi, ki, sm.T, t_valid=T, n_meta=n_meta, n_heads=HA, hd=HDA,
                                h_idx=h_idx, d_idx=d_idx, n_sel=n_sel_p, tq=tq, tk=tk_p)
                ob, s_new = gdn(proj, sm, conv0, s0, conv_w[l], ng, n_batch=1, rows_per_batch=Mp, t_valid=T,
                                n_heads=HB, hd=HDB, col0=3 * CA, b_off=b_off, a_off=a_off)
                nrow = T
            else:
                smt = sm.reshape(Bs, Ts, LANES).swapaxes(1, 2)
                oa = dsa_sample(proj, kcache, vcache, qi, ki, kicache, smt, l, n_batch=Bs, t_new=Ts,
                                n_heads=HA, hd=HDA, h_idx=h_idx, d_idx=d_idx, n_sel=n_sel_s, tk=tk_c)
                ob, s_new = gdn(proj, sm, state_conv[l], state_dn[l], conv_w[l], ng, n_batch=Bs, rows_per_batch=Ts,
                                t_valid=Ts, n_heads=HB, hd=HDB, col0=3 * CA, b_off=b_off, a_off=a_off)
                nrow = Ms
            x = merge_ln(h, hb, oa, ob, w_gate, woa, wob, wout, g3[1], b3[1], l, alpha, tm, tn_mg)
            x, _ = ffn_ln(x, wg, wu, wd, g3[2], b3[2], l, 1, alpha, tm, tf)
            nb = 1 if is_prompt else Bs
            per = nrow // nb
            qkv_b = proj[:nrow, 3 * CA:3 * CA + 3 * CD].reshape(nb, per, 3 * CD)
            if is_prompt:
                conv_new = qkv_b[:, per - (cw_len - 1):]
            else:
                conv_new = jnp.concatenate([state_conv[l], qkv_b], axis=1)[:, -(cw_len - 1):]
            streams.append((x, proj[:nrow, CA:2 * CA], proj[:nrow, 2 * CA:3 * CA], sm[:nrow, :d_idx], conv_new, s_new))
        (xp, *st_p), (xs, *st_s) = streams
        for i, a in enumerate(st_p):
            outs[i].append(a)
        for i, a in enumerate(st_s):
            outs[5 + i].append(a)
    y_prompt = xp[n_meta:T][None]
    y_sample = xs.reshape(Bs, Ts, D)
    res = []
    for i, o in enumerate(outs):
        a = jnp.stack(o)
        nb, per = (1, T) if i < 5 else (Bs, Ts)
        if i % 5 < 2:
            a = a.reshape(depth, nb, per, HA, HDA)
        elif i % 5 == 2:
            a = a.reshape(depth, nb, per, d_idx)
        res.append(a)
    return (y_prompt, y_sample) + tuple(res)
```

```python
import functools

import numpy as np
import jax
import jax.numpy as jnp
from jax import lax
from jax.experimental import pallas as pl
from jax.experimental.pallas import tpu as pltpu

CHUNK = 64
MAX_SEL = 256
DN_C = 64
LN_EPS = 1e-5
NORM_EPS = 1e-6

LANES = 128
SUBLANES = 8
KC = 256
FF_ALIGN = 512
VMEM_LIMIT = 56 << 20

ROW_PAD = 1280
TM_FFN = 640
TM_MM = 1280
TN_MM = 1024
TN_MERGE = 512
TQ = 256
TK_PROMPT = 1280
TK_CACHE = 1024

BF = jnp.bfloat16
F32 = jnp.float32
I32 = jnp.int32
I16 = jnp.int16
I16_OFF = 2 ** 15
PACK16 = 16
COUNT_UNROLL = 4
INT_MIN = -(2 ** 31)
NEG = -3.0e38
FAR = 2.0 ** 100
LOG2E = 1.4426950408889634
ONES_ROWS = 16
NT_DIMS = (((1,), (1,)), ((), ()))
TN_DIMS = (((0,), (0,)), ((), ()))


def _cparams(sem):
    return pltpu.CompilerParams(dimension_semantics=sem, vmem_limit_bytes=VMEM_LIMIT)


def _sigmoid(x):
    return 1.0 / (1.0 + jnp.exp(-x))


def _layer_norm(y, g, b):
    mu = jnp.mean(y, axis=-1, keepdims=True)
    d = y - mu
    var = jnp.mean(d * d, axis=-1, keepdims=True)
    return d * lax.rsqrt(var + LN_EPS) * g + b


def _pick_tile(n, pref):
    if n <= pref:
        return n
    for t in range(pref, 15, -16):
        if n % t == 0:
            return t
    return n


def _ffn_kernel(x_ref, wg_ref, wu_ref, wd_ref, g_ref, b_ref, o_ref, ob_ref, xb_ref, acc_ref, *, alpha, nf, rs):
    j = pl.program_id(1)

    @pl.when(j == 0)
    def _():
        xb_ref[...] = x_ref[...].astype(BF)
        acc_ref[...] = jnp.zeros_like(acc_ref)

    xb = xb_ref[...]
    hg = jnp.dot(xb, wg_ref[...], preferred_element_type=F32)
    hu = jnp.dot(xb, wu_ref[...], preferred_element_type=F32)
    a = (hg * _sigmoid(hg) * hu).astype(BF)
    acc_ref[...] += jnp.dot(a, wd_ref[...], preferred_element_type=F32)

    @pl.when(j == nf - 1)
    def _():
        def slab(r, carry):
            rows = pl.ds(pl.multiple_of(r * rs, rs), rs)
            y = alpha * x_ref[rows, :] + 0.5 * acc_ref[rows, :]
            z = _layer_norm(y, g_ref[...], b_ref[...])
            o_ref[rows, :] = z
            ob_ref[rows, :] = z.astype(BF)
            return carry
        lax.fori_loop(0, x_ref.shape[0] // rs, slab, 0)


def ffn_ln(x, wg, wu, wd, g, b, l, s, alpha, tm, tf):
    M, D = x.shape
    nf = wd.shape[2] // tf
    rs = 64 if tm % 64 == 0 else 16
    return pl.pallas_call(
        functools.partial(_ffn_kernel, alpha=alpha, nf=nf, rs=rs),
        grid=(M // tm, nf),
        in_specs=[
            pl.BlockSpec((tm, D), lambda i, j: (i, 0)),
            pl.BlockSpec((None, None, D, tf), lambda i, j: (l, s, 0, j)),
            pl.BlockSpec((None, None, D, tf), lambda i, j: (l, s, 0, j)),
            pl.BlockSpec((None, None, tf, D), lambda i, j: (l, s, j, 0)),
            pl.BlockSpec((1, D), lambda i, j: (0, 0)),
            pl.BlockSpec((1, D), lambda i, j: (0, 0)),
        ],
        out_specs=[pl.BlockSpec((tm, D), lambda i, j: (i, 0)), pl.BlockSpec((tm, D), lambda i, j: (i, 0))],
        out_shape=[jax.ShapeDtypeStruct((M, D), F32), jax.ShapeDtypeStruct((M, D), BF)],
        scratch_shapes=[pltpu.VMEM((tm, D), BF), pltpu.VMEM((tm, D), F32)],
        compiler_params=_cparams(("parallel", "arbitrary")),
        name="ffn_ln",
    )(x, wg, wu, wd, g, b)


def _mm_kernel(x_ref, w_ref, o_ref):
    o_ref[...] = jnp.dot(x_ref[...], w_ref[...], preferred_element_type=F32)


def matmul(xb, w, l, tm, tn):
    M, D = xb.shape
    N = w.shape[2]
    return pl.pallas_call(
        _mm_kernel,
        grid=(M // tm, N // tn),
        in_specs=[pl.BlockSpec((tm, D), lambda i, j: (i, 0)),
                  pl.BlockSpec((None, D, tn), lambda i, j: (l, 0, j))],
        out_specs=pl.BlockSpec((tm, tn), lambda i, j: (i, j)),
        out_shape=jax.ShapeDtypeStruct((M, N), F32),
        compiler_params=_cparams(("parallel", "arbitrary")),
        name="in_proj",
    )(xb, w)


def _idx_kernel(x_ref, w_ref, lng_ref, lnb_ref, alog_ref, dtb_ref, qi_ref, ki_ref, sm_ref, *, n_qi, d_idx, h_idx, h_b):
    r = jnp.dot(x_ref[...], w_ref[...], preferred_element_type=F32)
    qi_ref[...] = r[:, :n_qi].astype(BF)
    t = r[:, n_qi:]
    lane = lax.broadcasted_iota(I32, t.shape, 1)
    isk = lane < d_idx
    mu = jnp.sum(jnp.where(isk, t, 0.0), axis=-1, keepdims=True) / d_idx
    d = jnp.where(isk, t - mu, 0.0)
    var = jnp.sum(d * d, axis=-1, keepdims=True) / d_idx
    kn = d * lax.rsqrt(var + LN_EPS) * lng_ref[...] + lnb_ref[...]
    wv = t * (h_idx ** -0.5)
    beta = _sigmoid(t)
    z = t + dtb_ref[...]
    softplus = jnp.maximum(z, 0.0) + jnp.log(1.0 + jnp.exp(-jnp.abs(z)))
    la = -jnp.exp(alog_ref[...]) * softplus
    w_off, b_off, a_off = d_idx, d_idx + h_idx, d_idx + h_idx + h_b
    out = jnp.where(isk, kn,
                    jnp.where(lane < b_off, wv,
                              jnp.where(lane < a_off, beta,
                                        jnp.where(lane < a_off + h_b, la, 0.0))))
    sm_ref[...] = out
    ki_ref[...] = kn[:, :d_idx].astype(BF)


def idx_proj(xb, w, lng, lnb, alog, dtb, l, tm, d_idx, h_idx, h_b):
    M, D = xb.shape
    N = w.shape[2]
    n_qi = N - LANES
    vec = pl.BlockSpec((1, LANES), lambda i: (0, 0))
    return pl.pallas_call(
        functools.partial(_idx_kernel, n_qi=n_qi, d_idx=d_idx, h_idx=h_idx, h_b=h_b),
        grid=(M // tm,),
        in_specs=[pl.BlockSpec((tm, D), lambda i: (i, 0)),
                  pl.BlockSpec((None, D, N), lambda i: (l, 0, 0)),
                  vec, vec, vec, vec],
        out_specs=[pl.BlockSpec((tm, n_qi), lambda i: (i, 0)),
                   pl.BlockSpec((tm, d_idx), lambda i: (i, 0)),
                   pl.BlockSpec((tm, LANES), lambda i: (i, 0))],
        out_shape=[jax.ShapeDtypeStruct((M, n_qi), BF),
                   jax.ShapeDtypeStruct((M, d_idx), BF),
                   jax.ShapeDtypeStruct((M, LANES), F32)],
        compiler_params=_cparams(("parallel",)),
        name="idx_proj",
    )(xb, w, lng, lnb, alog, dtb)


def _sort_key(s):
    b = pltpu.bitcast(s + 0.0, I32)
    return b ^ ((b >> 31) & 0x7FFFFFFF)


def _transpose_to(dst_ref, x):
    tq = x.shape[0]
    if tq % LANES:
        x = jnp.concatenate([x, jnp.zeros((LANES - tq % LANES, x.shape[1]), F32)], axis=0)
    dst_ref[...] = x.T[:, :tq].astype(dst_ref.dtype)


def _index_scores(ki_chunk, qit_ref, w_rows, h_idx, d_idx):
    s = [jnp.dot(ki_chunk, qit_ref[h * d_idx:(h + 1) * d_idx, :], preferred_element_type=F32)
         for h in range(h_idx)]
    acc = w_rows[0:1, :] * jnp.maximum(s[0], 0.0)
    for h in range(1, h_idx):
        acc = acc + w_rows[h:h + 1, :] * jnp.maximum(s[h], 0.0)
    return acc


def _store_key(hi_ref, lo_ref, where, key):
    hi_ref[where] = (key >> 16).astype(I16)
    lo_ref[where] = (key ^ I16_OFF).astype(I16)


def _count16(nck, tq, pred):
    def body(i, cnt):
        parts = [jnp.where(pred(COUNT_UNROLL * i + u), jnp.int16(1), jnp.int16(0)) for u in range(COUNT_UNROLL)]
        while len(parts) > 1:
            parts = [a + b for a, b in zip(parts[::2], parts[1::2])]
        ones = parts[0]
        parts = [ones[r * PACK16:(r + 1) * PACK16] for r in range(ones.shape[0] // PACK16)]
        while len(parts) > 1:
            parts = [a + b for a, b in zip(parts[::2], parts[1::2])]
        return cnt + parts[0]
    cnt16 = lax.fori_loop(0, (nck + COUNT_UNROLL - 1) // COUNT_UNROLL, body, jnp.zeros((PACK16, tq), I16))
    return cnt16.astype(I32).sum(axis=0, keepdims=True)


def _mask_pad_chunk(hi_ref, lo_ref, nck):
    masked = jnp.full(hi_ref.shape[1:], -I16_OFF, I16)
    n_pad = (COUNT_UNROLL - nck % COUNT_UNROLL) % COUNT_UNROLL
    for r in range(COUNT_UNROLL - 1):
        if isinstance(nck, int):
            if r < n_pad:
                hi_ref[nck + r] = masked
                lo_ref[nck + r] = masked
        else:
            @pl.when(r < n_pad)
            def _():
                hi_ref[nck + r] = masked
                lo_ref[nck + r] = masked


def _search16(nck, tq, n_sel, base, cnt0, count_ge):
    def bit_body(i, carry):
        thr, cnt_thr = carry
        cand = thr + jnp.left_shift(jnp.int32(1), 15 - i)
        cnt = base + count_ge(cand.astype(I16))
        ok = cnt >= n_sel
        return jnp.where(ok, cand, thr), jnp.where(ok, cnt, cnt_thr)
    return lax.fori_loop(0, 16, bit_body, (jnp.full((1, tq), -I16_OFF, I32), cnt0))


def _select_top(hi_ref, lo_ref, nck, n_sel, tq, idx_bits, cut_ref):
    kc = hi_ref.shape[1]
    zero = jnp.zeros((1, tq), I32)
    _mask_pad_chunk(hi_ref, lo_ref, nck)
    thr_hi, cnt_ge_hi = _search16(nck, tq, n_sel, zero, zero,
                                  lambda t: _count16(nck, tq, lambda c: hi_ref[c] >= t))
    t_hi = thr_hi.astype(I16)
    cnt_gt_hi = _count16(nck, tq, lambda c: hi_ref[c] > t_hi)

    def lo_body(c, carry):
        lo_ref[c] = jnp.where(hi_ref[c] == t_hi, lo_ref[c], jnp.int16(-I16_OFF))
        return carry
    lax.fori_loop(0, nck, lo_body, 0)
    thr_lo, cnt_thr = _search16(nck, tq, n_sel, cnt_gt_hi, cnt_ge_hi,
                                lambda t: _count16(nck, tq, lambda c: lo_ref[c] >= t))
    t_lo = thr_lo.astype(I16)
    has_tie = (thr_hi > -I16_OFF) & (cnt_thr > n_sel)
    cut_ref[...] = jnp.full((1, tq), I16_OFF - 1, I32)

    def kidx16(c):
        return (c * kc + lax.broadcasted_iota(I32, (kc, tq), 0)).astype(I16)

    @pl.when(jnp.max(has_tie.astype(I32)) > 0)
    def _():
        need = n_sel - cnt_gt_hi - _count16(nck, tq, lambda c: (hi_ref[c] == t_hi) & (lo_ref[c] > t_lo))

        def idx_body(i, cut):
            cand = cut + jnp.left_shift(jnp.int32(1), idx_bits - 1 - i)
            c16 = cand.astype(I16)
            cnt = _count16(nck, tq, lambda c: (hi_ref[c] == t_hi) & (lo_ref[c] == t_lo) & (kidx16(c) < c16))
            return jnp.where(cnt < need, cand, cut)

        cut = lax.fori_loop(0, idx_bits, idx_body, zero)
        cut_ref[...] = jnp.where(has_tie, cut, I16_OFF - 1)

    cut = cut_ref[...].astype(I16)

    def bias_body(c, carry):
        hi, lo = hi_ref[c], lo_ref[c]
        sel = (hi > t_hi) | ((hi == t_hi) & ((lo > t_lo) | ((lo == t_lo) & (kidx16(c) <= cut))))
        sel = sel & (hi > jnp.int16(-I16_OFF))
        hi_ref[c] = pltpu.bitcast(jnp.where(sel, jnp.zeros((), BF), jnp.full((), FAR, BF)), I16)
        return carry
    lax.fori_loop(0, nck, bias_body, 0)


def _flash_scores(slot, far16, dist, k_heads, qt_ref, m_ref, lt_ref, p_ref, a_ref, n_heads, hd):
    kc = dist.shape[0]
    dist = dist + pltpu.bitcast(far16, BF).astype(F32)
    for h in range(n_heads):
        cols = slice(h * hd, (h + 1) * hd)
        lt_ref[h, 0:kc, :] = jnp.dot(k_heads[h](), qt_ref[cols, :], preferred_element_type=F32)
    for h in range(n_heads):
        slope = float(2.0 ** (-(8.0 / n_heads) * (h + 1)) * LOG2E)
        lt = lt_ref[h, 0:kc, :] - slope * dist
        m_old = m_ref[h]
        m_new = jnp.maximum(m_old, jnp.max(lt, axis=0, keepdims=True))
        p = jnp.exp2(lt - m_new)
        a = jnp.exp2(m_old - m_new)
        p_ref[slot, h, 0:kc, :] = p.astype(BF)
        a_ref[slot, h] = a
        m_ref[h] = m_new


def _flash_values(slot, kc, vt_heads, l_ref, acc_ref, p_ref, a_ref, n_heads, hd):
    for h in range(n_heads):
        pv = jnp.dot(vt_heads[h](), p_ref[slot, h, 0:kc, :], preferred_element_type=F32)
        a = a_ref[slot, h]
        acc_ref[h] = a * acc_ref[h] + pv[:hd]
        l_ref[h] = a * l_ref[h] + pv[hd:hd + 1]


def _flash_init(m_ref, l_ref, acc_ref):
    m_ref[...] = jnp.full(m_ref.shape, NEG, F32)
    l_ref[...] = jnp.zeros(l_ref.shape, F32)
    acc_ref[...] = jnp.zeros(acc_ref.shape, F32)


def _flash_finish(o_ref, l_ref, acc_ref, n_heads, hd, tq):
    for h in range(n_heads):
        ot = acc_ref[h] / l_ref[h]
        if tq % LANES:
            ot = jnp.concatenate([ot, jnp.zeros((hd, LANES - tq % LANES), F32)], axis=1)
        o_ref[:, h * hd:(h + 1) * hd] = ot.T[:tq].astype(BF)


def _dsa_prompt_kernel(qmap, kmap, nckq, nfull, q_ref, k_ref, vt_ref, qi_ref, ki_ref, smt_ref, o_ref,
                       hi_ref, lo_ref, cut_ref, qt_ref, qit_ref, w_ref, m_ref, l_ref, acc_ref, lt_ref, p_ref, a_ref,
                       *, n_heads, hd, h_idx, d_idx, n_meta, t_valid, n_sel, cpb, idx_bits):
    p = pl.program_id(0)
    qb, kb = qmap[p], kmap[p]
    nck = nckq[qb]
    tq = q_ref.shape[0]
    qpos = qb * tq + lax.broadcasted_iota(I32, (KC, tq), 1)

    @pl.when(kb == 0)
    def _():
        _transpose_to(qt_ref, q_ref[...] * (hd ** -0.5 * LOG2E))
        _transpose_to(qit_ref, qi_ref[...].astype(F32))
        w_ref[...] = smt_ref[d_idx:d_idx + SUBLANES, :] * (d_idx ** -0.5)
        qchk = jnp.maximum((qpos - n_meta) >> 6, -1)

        def score_body(c, carry, masked):
            k0 = pl.multiple_of(c * KC, KC)
            key = _sort_key(_index_scores(ki_ref[pl.ds(k0, KC), :], qit_ref, w_ref[...], h_idx, d_idx))
            if masked:
                kpos = k0 + lax.broadcasted_iota(I32, (KC, tq), 0)
                allowed = (jnp.maximum((kpos - n_meta) >> 6, -1) <= qchk) & (kpos < t_valid)
                key = jnp.where(allowed, key, INT_MIN)
            _store_key(hi_ref, lo_ref, c, key)
            return carry
        lax.fori_loop(0, nfull[qb], functools.partial(score_body, masked=False), 0)
        lax.fori_loop(nfull[qb], nck, functools.partial(score_body, masked=True), 0)
        _select_top(hi_ref, lo_ref, nck, n_sel, tq, idx_bits, cut_ref)
        _flash_init(m_ref, l_ref, acc_ref)

    vr = vt_ref.shape[1] // n_heads

    def chunk_body(c, carry):
        cg = kb * cpb + c
        rows = pl.ds(pl.multiple_of(c * KC, KC), KC)
        kpos = cg * KC + lax.broadcasted_iota(I32, (KC, tq), 0)
        dist = jnp.abs(qpos - kpos).astype(F32)
        k_heads = [functools.partial(lambda h: k_ref[rows, h * hd:(h + 1) * hd], h) for h in range(n_heads)]
        vt_heads = [functools.partial(lambda h: vt_ref[c, h * vr:(h + 1) * vr, :], h) for h in range(n_heads)]
        _flash_scores(0, hi_ref[cg], dist, k_heads, qt_ref, m_ref, lt_ref, p_ref, a_ref, n_heads, hd)
        _flash_values(0, KC, vt_heads, l_ref, acc_ref, p_ref, a_ref, n_heads, hd)
        return carry
    lax.fori_loop(0, jnp.minimum(cpb, nck - kb * cpb), chunk_body, 0)

    @pl.when((kb + 1) * cpb >= nck)
    def _():
        _flash_finish(o_ref, l_ref, acc_ref, n_heads, hd, tq)


def dsa_prompt(proj, kb16, vt3, qi, ki, smt, *, t_valid, n_meta, n_heads, hd, h_idx, d_idx, n_sel, tq, tk):
    Mp = proj.shape[0]
    C = n_heads * hd
    vr = vt3.shape[1]
    cpb = tk // KC
    nq = Mp // tq
    qlast = np.minimum(np.arange(nq) * tq + tq - 1, t_valid - 1)
    kend = np.minimum(t_valid, n_meta + CHUNK * (np.maximum((qlast - n_meta) // CHUNK, -1) + 1))
    nckq = -(-kend // KC)
    qfirst = np.arange(nq) * tq
    kend_first = np.minimum(t_valid, n_meta + CHUNK * (np.maximum((qfirst - n_meta) // CHUNK, -1) + 1))
    nfull = np.minimum(kend_first // KC, nckq)
    pairs = [(i, j) for i in range(nq) for j in range(-(-int(nckq[i]) // cpb))]
    qmap = jnp.asarray([a for a, _ in pairs], I32)
    kmap = jnp.asarray([b for _, b in pairs], I32)
    idx_bits = max(1, int(np.ceil(np.log2(Mp))))
    grid_spec = pltpu.PrefetchScalarGridSpec(
        num_scalar_prefetch=4,
        grid=(len(pairs),),
        in_specs=[
            pl.BlockSpec((tq, C), lambda p, qm, km, nk, nf: (qm[p], 0)),
            pl.BlockSpec((tk, C), lambda p, qm, km, nk, nf: (km[p], 0)),
            pl.BlockSpec((cpb, vr, KC), lambda p, qm, km, nk, nf: (km[p], 0, 0)),
            pl.BlockSpec((tq, h_idx * d_idx), lambda p, qm, km, nk, nf: (qm[p], 0)),
            pl.BlockSpec((Mp, d_idx), lambda p, qm, km, nk, nf: (0, 0)),
            pl.BlockSpec((LANES, tq), lambda p, qm, km, nk, nf: (0, qm[p])),
        ],
        out_specs=pl.BlockSpec((tq, C), lambda p, qm, km, nk, nf: (qm[p], 0)),
        scratch_shapes=[
            pltpu.VMEM((Mp // KC + COUNT_UNROLL - 1, KC, tq), I16),
            pltpu.VMEM((Mp // KC + COUNT_UNROLL - 1, KC, tq), I16),
            pltpu.VMEM((1, tq), I32),
            pltpu.VMEM((C, tq), BF), pltpu.VMEM((h_idx * d_idx, tq), BF), pltpu.VMEM((SUBLANES, tq), F32),
            pltpu.VMEM((n_heads, 1, tq), F32), pltpu.VMEM((n_heads, 1, tq), F32),
            pltpu.VMEM((n_heads, hd, tq), F32),
            pltpu.VMEM((n_heads, KC, tq), F32), pltpu.VMEM((2, n_heads, KC, tq), BF),
            pltpu.VMEM((2, n_heads, 1, tq), F32),
        ],
    )
    return pl.pallas_call(
        functools.partial(_dsa_prompt_kernel, n_heads=n_heads, hd=hd, h_idx=h_idx, d_idx=d_idx, n_meta=n_meta,
                          t_valid=t_valid, n_sel=n_sel, cpb=cpb, idx_bits=idx_bits),
        grid_spec=grid_spec,
        out_shape=jax.ShapeDtypeStruct((Mp, C), BF),
        compiler_params=_cparams(("arbitrary",)),
        name="dsa_prompt",
    )(qmap, kmap, jnp.asarray(nckq, I32), jnp.asarray(nfull, I32), proj, kb16, vt3, qi, ki, smt)


def _dsa_sample_kernel(q_ref, kn_ref, vn_ref, kc_ref, vc_ref, qi_ref, kin_ref, kic_ref, smt_ref, o_ref,
                       hi_ref, lo_ref, cut_ref, qt_ref, qit_ref, w_ref, m_ref, l_ref, acc_ref, lt_ref, p_ref, a_ref,
                       *, n_heads, hd, h_idx, d_idx, past, n_sel, cpb, nkb, idx_bits):
    kb = pl.program_id(1)
    tq = q_ref.shape[0]
    ncache = past // KC
    nck = ncache + 1
    qpos = past + lax.broadcasted_iota(I32, (KC, tq), 1)

    @pl.when(kb == 0)
    def _():
        _transpose_to(qt_ref, q_ref[...] * (hd ** -0.5 * LOG2E))
        _transpose_to(qit_ref, qi_ref[...].astype(F32))
        w_ref[...] = smt_ref[d_idx:d_idx + SUBLANES, :] * (d_idx ** -0.5)

        def score_body(c, carry):
            k0 = pl.multiple_of(c * KC, KC)
            sc = _index_scores(kic_ref[pl.ds(k0, KC), :], qit_ref, w_ref[...], h_idx, d_idx)
            _store_key(hi_ref, lo_ref, c, _sort_key(sc))
            return carry
        lax.fori_loop(0, ncache, score_body, 0)
        hi_ref[ncache] = jnp.full((KC, tq), -I16_OFF, I16)
        lo_ref[ncache] = jnp.full((KC, tq), -I16_OFF, I16)
        sc = _index_scores(kin_ref[...], qit_ref, w_ref[...], h_idx, d_idx)
        _store_key(hi_ref, lo_ref, (ncache, slice(0, tq), slice(None)), _sort_key(sc))
        _select_top(hi_ref, lo_ref, nck, n_sel, tq, idx_bits, cut_ref)
        _flash_init(m_ref, l_ref, acc_ref)

    def vt_of(v):
        kc = v.shape[0]
        if kc % LANES:
            v = jnp.concatenate([v, jnp.zeros((LANES - kc % LANES, hd), F32)], axis=0)
        return jnp.concatenate([v.T[:, :kc].astype(BF), jnp.ones((ONES_ROWS, kc), BF)], axis=0)

    vall = [None, None]
    for c in range(cpb):
        cg = kb * cpb + c
        kall = jnp.swapaxes(kc_ref[c * KC:(c + 1) * KC], 0, 1)
        kpos = cg * KC + lax.broadcasted_iota(I32, (KC, tq), 0)
        dist = jnp.abs(qpos - kpos).astype(F32)
        k_heads = [functools.partial(lambda h, kall=kall: kall[h].astype(BF), h) for h in range(n_heads)]
        if c > 0:
            vt_heads = [functools.partial(lambda h, v=vall[(c - 1) % 2]: vt_of(v[h]), h) for h in range(n_heads)]
            _flash_values((c - 1) % 2, KC, vt_heads, l_ref, acc_ref, p_ref, a_ref, n_heads, hd)
        vall[c % 2] = jnp.swapaxes(vc_ref[c * KC:(c + 1) * KC], 0, 1)
        _flash_scores(c % 2, hi_ref[cg], dist, k_heads, qt_ref, m_ref, lt_ref, p_ref, a_ref, n_heads, hd)
    vt_heads = [functools.partial(lambda h, v=vall[(cpb - 1) % 2]: vt_of(v[h]), h) for h in range(n_heads)]
    _flash_values((cpb - 1) % 2, KC, vt_heads, l_ref, acc_ref, p_ref, a_ref, n_heads, hd)

    @pl.when(kb == nkb - 1)
    def _():
        tn = kn_ref.shape[0]
        kpos = past + lax.broadcasted_iota(I32, (tn, tq), 0)
        dist = jnp.abs(past + lax.broadcasted_iota(I32, (tn, tq), 1) - kpos).astype(F32)
        k_heads = [functools.partial(lambda h: kn_ref[:, h * hd:(h + 1) * hd].astype(BF), h) for h in range(n_heads)]
        vt_heads = [functools.partial(lambda h: vt_of(vn_ref[:, h * hd:(h + 1) * hd]), h) for h in range(n_heads)]
        _flash_scores(0, hi_ref[ncache, 0:tn, :], dist, k_heads, qt_ref, m_ref, lt_ref, p_ref, a_ref, n_heads, hd)
        _flash_values(0, tn, vt_heads, l_ref, acc_ref, p_ref, a_ref, n_heads, hd)
        _flash_finish(o_ref, l_ref, acc_ref, n_heads, hd, tq)


def dsa_sample(proj, kcache, vcache, qi, ki, kicache, smt, l, *, n_batch, t_new, n_heads, hd, h_idx, d_idx, n_sel, tk):
    C = n_heads * hd
    past = kcache.shape[2]
    cpb = tk // KC
    nkb = past // tk
    idx_bits = max(1, int(np.ceil(np.log2(past + KC))))
    tq = t_new
    return pl.pallas_call(
        functools.partial(_dsa_sample_kernel, n_heads=n_heads, hd=hd, h_idx=h_idx, d_idx=d_idx, past=past,
                          n_sel=n_sel, cpb=cpb, nkb=nkb, idx_bits=idx_bits),
        grid=(n_batch, nkb),
        in_specs=[
            pl.BlockSpec((tq, C), lambda b, j: (b, 0)),
            pl.BlockSpec((tq, C), lambda b, j: (b, 1)),
            pl.BlockSpec((tq, C), lambda b, j: (b, 2)),
            pl.BlockSpec((None, None, tk, n_heads, hd), lambda b, j: (l, b, j, 0, 0)),
            pl.BlockSpec((None, None, tk, n_heads, hd), lambda b, j: (l, b, j, 0, 0)),
            pl.BlockSpec((tq, h_idx * d_idx), lambda b, j: (b, 0)),
            pl.BlockSpec((tq, d_idx), lambda b, j: (b, 0)),
            pl.BlockSpec((None, None, past, d_idx), lambda b, j: (l, b, 0, 0)),
            pl.BlockSpec((None, LANES, tq), lambda b, j: (b, 0, 0)),
        ],
        out_specs=pl.BlockSpec((tq, C), lambda b, j: (b, 0)),
        out_shape=jax.ShapeDtypeStruct((n_batch * tq, C), BF),
        scratch_shapes=[
            pltpu.VMEM((past // KC + COUNT_UNROLL, KC, tq), I16),
            pltpu.VMEM((past // KC + COUNT_UNROLL, KC, tq), I16),
            pltpu.VMEM((1, tq), I32),
            pltpu.VMEM((C, tq), BF), pltpu.VMEM((h_idx * d_idx, tq), BF), pltpu.VMEM((SUBLANES, tq), F32),
            pltpu.VMEM((n_heads, 1, tq), F32), pltpu.VMEM((n_heads, 1, tq), F32),
            pltpu.VMEM((n_heads, hd, tq), F32),
            pltpu.VMEM((n_heads, KC, tq), F32), pltpu.VMEM((2, n_heads, KC, tq), BF),
            pltpu.VMEM((2, n_heads, 1, tq), F32),
        ],
        compiler_params=_cparams(("parallel", "arbitrary")),
        name="dsa_sample",
    )(proj, proj, proj, kcache, vcache, qi, ki, kicache, smt)


def _split3(x):
    hi = x.astype(BF)
    r = x - hi.astype(F32)
    mid = r.astype(BF)
    lo = (r - mid.astype(F32)).astype(BF)
    return hi, mid, lo


def _gdn_kernel(x_ref, gb_ref, sm_ref, cp_ref, s0_ref, cw_ref, ng_ref, ob_ref, sn_ref, S_ref, cbuf_ref,
                *, n_heads, hd, t_valid, conv_w, b_off, a_off, nchunks):
    c = pl.program_id(1)
    C = x_ref.shape[0]
    cd = n_heads * hd
    halo = conv_w - 1

    @pl.when(c == 0)
    def _():
        S_ref[...] = s0_ref[...]
        cbuf_ref[SUBLANES - halo:SUBLANES, :] = cp_ref[...]

    cbuf_ref[SUBLANES:SUBLANES + C, :] = x_ref[...]
    conv = None
    for i in range(conv_w):
        term = cbuf_ref[SUBLANES - halo + i:SUBLANES - halo + i + C, :] * cw_ref[i:i + 1, :]
        conv = term if conv is None else conv + term
    cbuf_ref[0:SUBLANES, :] = cbuf_ref[C:C + SUBLANES, :]
    act = conv * _sigmoid(conv)

    sm = sm_ref[...]
    row = c * C + lax.broadcasted_iota(I32, (C, LANES), 0)
    lane = lax.broadcasted_iota(I32, (C, LANES), 1)
    valid = row < t_valid
    la = jnp.where(valid & (lane >= a_off) & (lane < a_off + n_heads), sm, 0.0)
    beta_t = jnp.where(valid, sm, 0.0)
    ri = lax.broadcasted_iota(I32, (C, C), 0)
    ci = lax.broadcasted_iota(I32, (C, C), 1)
    causal = ri >= ci
    strict = ri > ci
    ltri = jnp.where(causal, 1.0, 0.0).astype(BF)
    g_t = sum(jnp.dot(ltri, piece, preferred_element_type=F32) for piece in _split3(la))
    g_rows = jnp.concatenate([g_t, jnp.zeros((LANES - C, LANES), F32)], axis=0).T

    H = range(n_heads)
    dot = functools.partial(jnp.dot, preferred_element_type=F32)
    dot_nt = functools.partial(lax.dot_general, dimension_numbers=NT_DIMS, preferred_element_type=F32)
    qh = [act[:, h * hd:(h + 1) * hd] for h in H]
    kh = [act[:, cd + h * hd:cd + (h + 1) * hd] for h in H]
    vh = [act[:, 2 * cd + h * hd:2 * cd + (h + 1) * hd] for h in H]
    qn = [qh[h] * lax.rsqrt(jnp.sum(qh[h] * qh[h], axis=-1, keepdims=True) + NORM_EPS) * (hd ** -0.5) for h in H]
    kn = [kh[h] * lax.rsqrt(jnp.sum(kh[h] * kh[h], axis=-1, keepdims=True) + NORM_EPS) for h in H]
    beta = [beta_t[:, b_off + h:b_off + h + 1] for h in H]
    g_c = [g_t[:, a_off + h:a_off + h + 1] for h in H]
    g_r = [g_rows[a_off + h:a_off + h + 1, :C] for h in H]
    g_last = [g_t[C - 1:C, a_off + h:a_off + h + 1] for h in H]
    decay = [jnp.where(causal, jnp.exp(jnp.where(causal, g_c[h] - g_r[h], 0.0)), 0.0) for h in H]
    eg = [jnp.exp(g_c[h]) for h in H]
    kb = [kn[h].astype(BF) for h in H]
    qb = [qn[h].astype(BF) for h in H]
    Sb = [S_ref[h].astype(BF) for h in H]
    kk = [dot_nt(kb[h], kb[h]) for h in H]
    qk = [dot_nt(qb[h], kb[h]) * decay[h] for h in H]
    o_state = [dot((qn[h] * eg[h]).astype(BF), Sb[h]) for h in H]
    N = [jnp.where(strict, -(beta[h] * kk[h] * decay[h]), 0.0) for h in H]
    P = list(N)
    for _ in range(int(np.log2(C)) - 1):
        Pb = [P[h].astype(BF) for h in H]
        P = [dot(Pb[h], Pb[h]) for h in H]
        NP = [dot(N[h].astype(BF), P[h].astype(BF)) for h in H]
        N = [N[h] + P[h] + NP[h] for h in H]
    rhs = [jnp.concatenate([vh[h] * beta[h], kn[h] * (beta[h] * eg[h])], axis=1) for h in H]
    corr = [dot(N[h].astype(BF), rhs[h].astype(BF)) for h in H]
    sol = [rhs[h] + corr[h] for h in H]
    su = [dot(sol[h][:, hd:].astype(BF), Sb[h]) for h in H]
    ub = [(sol[h][:, :hd] - su[h]).astype(BF) for h in H]
    o = [o_state[h] + dot(qk[h].astype(BF), ub[h]) for h in H]
    kd = [(kn[h] * jnp.exp(g_last[h] - g_c[h])).astype(BF) for h in H]
    upd = [lax.dot_general(kd[h], ub[h], TN_DIMS, preferred_element_type=F32) for h in H]
    for h in H:
        cols = slice(h * hd, (h + 1) * hd)
        S_ref[h] = S_ref[h] * jnp.exp(g_last[h]) + upd[h]
        gh = gb_ref[:, cols]
        on = o[h] * lax.rsqrt(jnp.mean(o[h] * o[h], axis=-1, keepdims=True) + NORM_EPS) * ng_ref[...]
        ob_ref[:, cols] = (on * (gh * _sigmoid(gh))).astype(BF)

    @pl.when(c == nchunks - 1)
    def _():
        sn_ref[...] = S_ref[...]


def gdn(proj, sm, conv_prev, s0, cw, ng, *, n_batch, rows_per_batch, t_valid, n_heads, hd, col0, b_off, a_off):
    cd = n_heads * hd
    nchunks = rows_per_batch // DN_C
    conv_w = cw.shape[0]
    qkv_blk = col0 // (3 * cd)
    g_blk = (col0 + 3 * cd) // cd
    return pl.pallas_call(
        functools.partial(_gdn_kernel, n_heads=n_heads, hd=hd, t_valid=t_valid, conv_w=conv_w,
                          b_off=b_off, a_off=a_off, nchunks=nchunks),
        grid=(n_batch, nchunks),
        in_specs=[
            pl.BlockSpec((DN_C, 3 * cd), lambda b, c: (b * nchunks + c, qkv_blk)),
            pl.BlockSpec((DN_C, cd), lambda b, c: (b * nchunks + c, g_blk)),
            pl.BlockSpec((DN_C, LANES), lambda b, c: (b * nchunks + c, 0)),
            pl.BlockSpec((None, conv_w - 1, 3 * cd), lambda b, c: (b, 0, 0)),
            pl.BlockSpec((None, n_heads, hd, hd), lambda b, c: (b, 0, 0, 0)),
            pl.BlockSpec((conv_w, 3 * cd), lambda b, c: (0, 0)),
            pl.BlockSpec((1, hd), lambda b, c: (0, 0)),
        ],
        out_specs=[pl.BlockSpec((DN_C, cd), lambda b, c: (b * nchunks + c, 0)),
                   pl.BlockSpec((None, n_heads, hd, hd), lambda b, c: (b, 0, 0, 0))],
        out_shape=[jax.ShapeDtypeStruct((n_batch * rows_per_batch, cd), BF),
                   jax.ShapeDtypeStruct((n_batch, n_heads, hd, hd), F32)],
        scratch_shapes=[pltpu.VMEM((n_heads, hd, hd), F32),
                        pltpu.VMEM((DN_C + SUBLANES, 3 * cd), F32)],
        compiler_params=_cparams(("parallel", "arbitrary")),
        name="gdn",
    )(proj, proj, sm, conv_prev, s0, cw, ng)


def _merge_kernel(h_ref, hb_ref, oa_ref, ob_ref, wga_ref, wgb_ref, woa_ref, wob_ref, wout_ref, g_ref, b_ref,
                  o_ref, acc_ref, *, alpha, nn, rs):
    j = pl.program_id(1)

    @pl.when(j == 0)
    def _():
        acc_ref[...] = jnp.zeros_like(acc_ref)

    hb = hb_ref[...]
    ga = _sigmoid(jnp.dot(hb, wga_ref[...], preferred_element_type=F32))
    gb = _sigmoid(jnp.dot(hb, wgb_ref[...], preferred_element_type=F32))
    y = (ga * jnp.dot(oa_ref[...], woa_ref[...], preferred_element_type=F32)
         + gb * jnp.dot(ob_ref[...], wob_ref[...], preferred_element_type=F32))
    acc_ref[...] += jnp.dot(y.astype(BF), wout_ref[...], preferred_element_type=F32)

    @pl.when(j == nn - 1)
    def _():
        def slab(r, carry):
            rows = pl.ds(pl.multiple_of(r * rs, rs), rs)
            o_ref[rows, :] = _layer_norm(alpha * h_ref[rows, :] + acc_ref[rows, :], g_ref[...], b_ref[...])
            return carry
        lax.fori_loop(0, h_ref.shape[0] // rs, slab, 0)


def merge_ln(h, hb, oa, ob, wgate, woa, wob, wout, g, b, l, alpha, tm, tn):
    M, D = h.shape
    ca, cb = oa.shape[1], ob.shape[1]
    nn = D // tn
    rs = 64 if tm % 64 == 0 else 16
    return pl.pallas_call(
        functools.partial(_merge_kernel, alpha=alpha, nn=nn, rs=rs),
        grid=(M // tm, nn),
        in_specs=[
            pl.BlockSpec((tm, D), lambda i, j: (i, 0)),
            pl.BlockSpec((tm, D), lambda i, j: (i, 0)),
            pl.BlockSpec((tm, ca), lambda i, j: (i, 0)),
            pl.BlockSpec((tm, cb), lambda i, j: (i, 0)),
            pl.BlockSpec((None, D, tn), lambda i, j: (l, 0, j)),
            pl.BlockSpec((None, D, tn), lambda i, j: (l, 0, nn + j)),
            pl.BlockSpec((None, ca, tn), lambda i, j: (l, 0, j)),
            pl.BlockSpec((None, cb, tn), lambda i, j: (l, 0, j)),
            pl.BlockSpec((None, tn, D), lambda i, j: (l, j, 0)),
            pl.BlockSpec((1, D), lambda i, j: (0, 0)),
            pl.BlockSpec((1, D), lambda i, j: (0, 0)),
        ],
        out_specs=pl.BlockSpec((tm, D), lambda i, j: (i, 0)),
        out_shape=jax.ShapeDtypeStruct((M, D), F32),
        scratch_shapes=[pltpu.VMEM((tm, D), F32)],
        compiler_params=_cparams(("parallel", "arbitrary")),
        name="merge_ln",
    )(h, hb, oa, ob, wgate, wgate, woa, wob, wout, g, b)


def _round_up(n, m):
    return -(-n // m) * m


def kernel(x_prompt, x_sample, cache_k, cache_v, cache_kidx, state_conv, state_dn, meta, w_ff_gu, w_ff_down, ln_g, ln_b, w_in, conv_w, idx_ln_g, idx_ln_b, a_log, dt_bias, dn_norm_g, w_oa, w_ob, w_out):
    depth = w_in.shape[0]
    B, S, D = x_prompt.shape
    Bs, Ts, _ = x_sample.shape
    n_meta = meta.shape[0]
    past, HA, HDA = cache_k.shape[2], cache_k.shape[3], cache_k.shape[4]
    CA = HA * HDA
    d_idx = cache_kidx.shape[-1]
    HB, HDB = state_dn.shape[2], state_dn.shape[3]
    CD = HB * HDB
    cw_len = conv_w.shape[1]
    F = w_ff_down.shape[2]
    n_in = w_in.shape[2]
    h_idx = (n_in - 3 * CA - d_idx - 4 * CD - 2 * HB - 2 * D) // (d_idx + 1)
    assert B == 1 and Ts == DN_C and CHUNK == 64 and n_meta <= CHUNK
    assert d_idx + h_idx + 2 * HB <= LANES and h_idx <= SUBLANES and HA * HDA == CA
    assert past % KC == 0 and CA == CD
    alpha = float((2 * depth) ** 0.25)
    T = n_meta + S
    Mp = _round_up(T, ROW_PAD)
    n_sel_p = min(MAX_SEL, S // 4)
    n_sel_s = min(MAX_SEL, (past + Ts) // 4)
    w_off, b_off, a_off = d_idx, d_idx + h_idx, d_idx + h_idx + HB

    Fp = _round_up(F, FF_ALIGN)
    padf = ((0, 0), (0, 0), (0, 0), (0, Fp - F))
    wg = jnp.pad(w_ff_gu[..., :F].astype(BF), padf)
    wu = jnp.pad(w_ff_gu[..., F:].astype(BF), padf)
    wd = jnp.pad(w_ff_down.astype(BF), ((0, 0), (0, 0), (0, Fp - F), (0, 0)))
    offs = np.cumsum([0, CA, CA, CA, h_idx * d_idx, d_idx, h_idx, 3 * CD, CD, HB, HB, D, D])
    span = lambda i, j: w_in[:, :, offs[i]:offs[j]].astype(BF)
    w_main = jnp.concatenate([span(0, 3), span(6, 8)], axis=2)
    pad_small = LANES - (d_idx + h_idx + 2 * HB)
    w_idx = jnp.concatenate([span(3, 6), span(8, 10), jnp.zeros((depth, D, pad_small), BF)], axis=2)
    w_gate = span(10, 12)
    woa, wob, wout = w_oa.astype(BF), w_ob.astype(BF), w_out.astype(BF)

    def lane_vec(v, off):
        return jnp.zeros((depth, 1, LANES), F32).at[:, 0, off:off + v.shape[1]].set(v)
    lng_v, lnb_v = lane_vec(idx_ln_g, 0), lane_vec(idx_ln_b, 0)
    alog_v, dtb_v = lane_vec(a_log, a_off), lane_vec(dt_bias, a_off)

    xp = jnp.concatenate([meta.astype(F32), x_prompt[0], jnp.zeros((Mp - T, D), F32)], axis=0)
    xs = x_sample.reshape(Bs * Ts, D)
    Ms = Bs * Ts
    tm_p, tm_s = _pick_tile(Mp, TM_FFN), _pick_tile(Ms, 512)
    tmm_p, tmm_s = _pick_tile(Mp, TM_MM), _pick_tile(Ms, 1024)
    tf = FF_ALIGN
    tn_mm = _pick_tile(w_main.shape[2], TN_MM)
    tn_mg = _pick_tile(D, TN_MERGE)
    tq = _pick_tile(Mp, TQ)
    tk_p = _pick_tile(Mp, TK_PROMPT)
    tk_c = _pick_tile(past, TK_CACHE)
    assert tq % LANES == 0 and tk_p % KC == 0 and tk_c % KC == 0 and Mp % KC == 0

    kcache, vcache = cache_k, cache_v
    kicache = cache_kidx.astype(BF)
    conv0 = jnp.zeros((1, cw_len - 1, 3 * CD), F32)
    s0 = jnp.zeros((1, HB, HDB, HDB), F32)

    outs = [[] for _ in range(10)]
    for l in range(depth):
        g3 = [ln_g[l, i][None] for i in range(3)]
        b3 = [ln_b[l, i][None] for i in range(3)]
        ng = dn_norm_g[l][None]
        streams = []
        for (x, is_prompt) in ((xp, True), (xs, False)):
            tm, tmm = (tm_p, tmm_p) if is_prompt else (tm_s, tmm_s)
            h, hb = ffn_ln(x, wg, wu, wd, g3[0], b3[0], l, 0, alpha, tm, tf)
            proj = matmul(hb, w_main, l, tmm, tn_mm)
            qi, ki, sm = idx_proj(hb, w_idx, lng_v[l], lnb_v[l], alog_v[l], dtb_v[l], l, tmm, d_idx, h_idx, HB)
            if is_prompt:
                kb16 = proj[:, CA:2 * CA].astype(BF)
                vt4 = proj[:, 2 * CA:3 * CA].astype(BF).reshape(Mp // KC, KC, HA, HDA).transpose(0, 2, 3, 1)
                vt3 = jnp.concatenate([vt4, jnp.ones((Mp // KC, HA, ONES_ROWS, KC), BF)], axis=2)
                vt3 = vt3.reshape(Mp // KC, HA * (HDA + ONES_ROWS), KC)
                oa = dsa_prompt(proj, kb16, vt3, qi, ki, sm.T, t_valid=T, n_meta=n_meta, n_heads=HA, hd=HDA,
                                h_idx=h_idx, d_idx=d_idx, n_sel=n_sel_p, tq=tq, tk=tk_p)
                ob, s_new = gdn(proj, sm, conv0, s0, conv_w[l], ng, n_batch=1, rows_per_batch=Mp, t_valid=T,
                                n_heads=HB, hd=HDB, col0=3 * CA, b_off=b_off, a_off=a_off)
                nrow = T
            else:
                smt = sm.reshape(Bs, Ts, LANES).swapaxes(1, 2)
                oa = dsa_sample(proj, kcache, vcache, qi, ki, kicache, smt, l, n_batch=Bs, t_new=Ts,
                                n_heads=HA, hd=HDA, h_idx=h_idx, d_idx=d_idx, n_sel=n_sel_s, tk=tk_c)
                ob, s_new = gdn(proj, sm, state_conv[l], state_dn[l], conv_w[l], ng, n_batch=Bs, rows_per_batch=Ts,
                                t_valid=Ts, n_heads=HB, hd=HDB, col0=3 * CA, b_off=b_off, a_off=a_off)
                nrow = Ms
            x = merge_ln(h, hb, oa, ob, w_gate, woa, wob, wout, g3[1], b3[1], l, alpha, tm, tn_mg)
            x, _ = ffn_ln(x, wg, wu, wd, g3[2], b3[2], l, 1, alpha, tm, tf)
            nb = 1 if is_prompt else Bs
            per = nrow // nb
            qkv_b = proj[:nrow, 3 * CA:3 * CA + 3 * CD].reshape(nb, per, 3 * CD)
            if is_prompt:
                conv_new = qkv_b[:, per - (cw_len - 1):]
            else:
                conv_new = jnp.concatenate([state_conv[l], qkv_b], axis=1)[:, -(cw_len - 1):]
            streams.append((x, proj[:nrow, CA:2 * CA], proj[:nrow, 2 * CA:3 * CA], sm[:nrow, :d_idx], conv_new, s_new))
        (xp, *st_p), (xs, *st_s) = streams
        for i, a in enumerate(st_p):
            outs[i].append(a)
        for i, a in enumerate(st_s):
            outs[5 + i].append(a)
    y_prompt = xp[n_meta:T][None]
    y_sample = xs.reshape(Bs, Ts, D)
    res = []
    for i, o in enumerate(outs):
        a = jnp.stack(o)
        nb, per = (1, T) if i < 5 else (Bs, Ts)
        if i % 5 < 2:
            a = a.reshape(depth, nb, per, HA, HDA)
        elif i % 5 == 2:
            a = a.reshape(depth, nb, per, d_idx)
        res.append(a)
    return (y_prompt, y_sample) + tuple(res)
```
